```python
import math, functools
import jax, jax.numpy as jnp
from jax import lax
import numpy as np

D_MODEL = 2048
BATCH = 16
SEQ = 2048
DEPTH = 4

GRID_W = 64
CTX_LEN = 256
MIX_HALF = D_MODEL // 2
A_HEADS = 8
A_HEAD_K = 128
A_HEAD_V = MIX_HALF // A_HEADS
A_KEY = A_HEADS * A_HEAD_K
A_VAL = A_HEADS * A_HEAD_V
A_CHUNK = 32
B_HEADS = 8
B_HEAD_K = 128
B_HEAD_V = MIX_HALF // B_HEADS
B_KEY = B_HEADS * B_HEAD_K
B_VAL = B_HEADS * B_HEAD_V
B_CONV_W = 5
B_CHUNK = 64
C_HEADS = 16
C_HEAD_DIM = D_MODEL // C_HEADS
WIN_R = 8
WIN_C = 16
QB_C = 16
ROPE_BASE = 10000.0
D_FF = ((8 * D_MODEL // 3 + 127) // 128) * 128
FFN_CONV_W = 3
ALPHA = (2 * DEPTH) ** 0.25
BETA = (8 * DEPTH) ** -0.25
N_EVEN = (DEPTH + 1) // 2
N_ODD = DEPTH // 2
EPS = 1e-6
AB_SIZES = (A_KEY, A_KEY, A_KEY, A_VAL, A_VAL, B_KEY, B_KEY, B_VAL, B_VAL, 4 * B_HEADS)
AB_SPLITS = tuple(int(v) for v in np.cumsum(AB_SIZES)[:-1])
N_IN_AB = int(sum(AB_SIZES))

kernel_name = 'hybrid_hgrn2_gdn_natten_convglu_deepnorm'

F32 = jnp.float32


def _layernorm(x, g, b):
    xf = x.astype(F32)
    xc = xf - jnp.mean(xf, -1, keepdims=True)
    var = jnp.mean(xc * xc, -1, keepdims=True)
    return (xc * lax.rsqrt(var + EPS) * g.astype(F32) + b.astype(F32)).astype(x.dtype)


def _rmsnorm(x, w):
    xf = x.astype(F32)
    return xf * lax.rsqrt(jnp.mean(xf * xf, -1, keepdims=True) + EPS) * w.astype(F32)


def _l2norm(x):
    xf = x.astype(F32)
    return xf * lax.rsqrt(jnp.sum(xf * xf, -1, keepdims=True) + EPS)


def _heads(t, h):
    return t.reshape(t.shape[:-1] + (h, t.shape[-1] // h))


def _dwconv(x, w):
    pad = w.shape[0] // 2
    return lax.conv_general_dilated(x, w[:, None, :].astype(x.dtype), (1,), [(pad, pad)],
                                    dimension_numbers=('NWC', 'WIO', 'NWC'),
                                    feature_group_count=x.shape[-1])


def _to_chunks(t, chunk):
    b, l, h = t.shape[:3]
    t = t.astype(F32).reshape((b, l // chunk, chunk, h) + t.shape[3:])
    return jnp.swapaxes(jnp.moveaxis(t, 1, 0), 2, 3)


def _from_chunks(t):
    n, b, h, c = t.shape[:4]
    t = jnp.moveaxis(jnp.swapaxes(t, 2, 3), 0, 1)
    return t.reshape((b, n * c, h) + t.shape[4:])


def _gla_chunked(q, k, v, log_f, s0, chunk):
    qc, kc, vc, gc = (_to_chunks(t, chunk) for t in (q, k, v, log_f))
    cum = jnp.cumsum(gc, axis=3)
    ref = cum[:, :, :, chunk // 2 - 1:chunk // 2]
    causal = jnp.tril(jnp.ones((chunk, chunk), bool))

    def step(s, inp):
        q_, k_, v_, b_, m_ = inp
        a = jnp.einsum('bhtd,bhsd->bhts', q_ * jnp.exp(b_ - m_), k_ * jnp.exp(m_ - b_))
        a = jnp.where(causal, a, 0.0)
        o = (jnp.einsum('bhtd,bhdv->bhtv', q_ * jnp.exp(b_), s)
             + jnp.einsum('bhts,bhsv->bhtv', a, v_))
        b_last = b_[:, :, -1:, :]
        s = (s * jnp.swapaxes(jnp.exp(b_last), 2, 3)
             + jnp.einsum('bhsd,bhsv->bhdv', k_ * jnp.exp(b_last - b_), v_))
        return s, o

    s_fin, o = lax.scan(step, s0, (qc, kc, vc, cum, ref))
    return _from_chunks(o).astype(v.dtype), s_fin


def _gdn_chunked(q, k, v, g, beta, s0, chunk):
    qc, kc, vc, gc, bc = (_to_chunks(t, chunk) for t in (q, k, v, g, beta))
    gc = jnp.cumsum(gc, axis=-1)
    dv = v.shape[-1]
    incl = jnp.tril(jnp.ones((chunk, chunk), bool))
    strict = jnp.tril(jnp.ones((chunk, chunk), bool), -1)
    eye = jnp.eye(chunk, dtype=F32)

    def step(s, inp):
        q_, k_, v_, g_, b_ = inp
        decay = jnp.exp(jnp.where(incl, g_[..., :, None] - g_[..., None, :], -jnp.inf))
        kb = k_ * b_[..., None]
        lmat = jnp.where(strict, jnp.einsum('bhtd,bhsd->bhts', kb, k_) * decay, 0.0)
        rhs = jnp.concatenate([v_ * b_[..., None], kb * jnp.exp(g_)[..., None]], -1)
        sol = lax.linalg.triangular_solve(eye + lmat, rhs, left_side=True, lower=True,
                                          unit_diagonal=True)
        u, w = sol[..., :dv], sol[..., dv:]
        v_new = u - jnp.einsum('bhtd,bhdv->bhtv', w, s)
        attn = jnp.einsum('bhtd,bhsd->bhts', q_, k_) * decay
        o = (jnp.einsum('bhtd,bhdv->bhtv', q_ * jnp.exp(g_)[..., None], s)
             + jnp.einsum('bhts,bhsv->bhtv', attn, v_new))
        g_last = g_[..., -1:]
        s = (s * jnp.exp(g_last)[..., None]
             + jnp.einsum('bhsd,bhsv->bhdv', k_ * jnp.exp(g_last - g_)[..., None], v_new))
        return s, o

    s_fin, o = lax.scan(step, s0, (qc, kc, vc, gc, bc))
    return _from_chunks(o).astype(v.dtype), s_fin


def _ctx_then_lat(scan_fn, ctx_in, lat_in, reverse):
    if reverse:
        ctx_in = tuple(jnp.flip(t, 1) for t in ctx_in)
        lat_in = tuple(jnp.flip(t, 1) for t in lat_in)
    b, _, h, dk = ctx_in[1].shape
    s0 = jnp.zeros((b, h, dk, ctx_in[2].shape[-1]), F32)
    o_ctx, s_ctx = scan_fn(*ctx_in, s0)
    o_lat, _ = scan_fn(*lat_in, s_ctx)
    if reverse:
        o_ctx, o_lat = jnp.flip(o_ctx, 1), jnp.flip(o_lat, 1)
    return o_ctx, o_lat


def _mix_ab(h_ctx, h_lat, w_in, lb, a_norm, conv_w, a_log, dt_bias, b_norm, w_out, need_ctx):
    gla = functools.partial(_gla_chunked, chunk=A_CHUNK)
    gdn = functools.partial(_gdn_chunked, chunk=B_CHUNK)

    def prepare(h):
        (qa, fa_f, fa_b, ia, ga, qb, kb, vb, zb, gb) = jnp.split(h @ w_in, AB_SPLITS, axis=-1)
        qa = _heads(jax.nn.silu(qa), A_HEADS)
        ia = _heads(ia, A_HEADS)
        a_dirs = []
        for d, zf in enumerate((fa_f, fa_b)):
            f = lb[d] + (1.0 - lb[d]) * jax.nn.sigmoid(zf.astype(F32))
            a_dirs.append((qa, _heads(1.0 - f, A_HEADS), ia, _heads(jnp.log(f), A_HEADS)))
        qkv = jax.nn.silu(_dwconv(jnp.concatenate([qb, kb, vb], -1), conv_w))
        qb, kb, vb = jnp.split(qkv, [B_KEY, 2 * B_KEY], axis=-1)
        qb = _l2norm(_heads(qb, B_HEADS)) * B_HEAD_K ** -0.5
        kb = _l2norm(_heads(kb, B_HEADS))
        vb = _heads(vb, B_HEADS)
        a_f, a_b, beta_f, beta_b = jnp.split(gb.astype(F32), 4, axis=-1)
        b_dirs = []
        for d, (a, beta) in enumerate(((a_f, beta_f), (a_b, beta_b))):
            g = -jnp.exp(a_log[d].astype(F32)) * jax.nn.softplus(a + dt_bias[d].astype(F32))
            b_dirs.append((qb, kb, vb, g, jax.nn.sigmoid(beta)))
        return a_dirs, b_dirs, ga, zb

    a_ctx, b_ctx, ga_ctx, zb_ctx = prepare(h_ctx)
    a_lat, b_lat, ga_lat, zb_lat = prepare(h_lat)
    oa = [_ctx_then_lat(gla, a_ctx[d], a_lat[d], d == 1) for d in range(2)]
    ob = [_ctx_then_lat(gdn, b_ctx[d], b_lat[d], d == 1) for d in range(2)]

    def merge(o_a, o_b, ga, zb):
        ya = _rmsnorm(o_a, a_norm) * jax.nn.silu(_heads(ga, A_HEADS).astype(F32))
        yb = _rmsnorm(o_b, b_norm) * jax.nn.silu(_heads(zb, B_HEADS).astype(F32))
        y = jnp.concatenate([ya.reshape(ya.shape[:2] + (A_VAL,)),
                             yb.reshape(yb.shape[:2] + (B_VAL,))], -1)
        return y.astype(w_out.dtype) @ w_out

    y_lat = merge(oa[0][1] + oa[1][1], ob[0][1] + ob[1][1], ga_lat, zb_lat)
    y_ctx = None
    if need_ctx:
        y_ctx = merge(oa[0][0] + oa[1][0], ob[0][0] + ob[1][0], ga_ctx, zb_ctx)
    return y_ctx, y_lat


def _axial_rope(x, row, col):
    half = x.shape[-1] // 2
    quarter = half // 2
    inv = ROPE_BASE ** (-jnp.arange(quarter, dtype=F32) / quarter)

    def rot(t, pos):
        ang = pos[:, None] * inv[None, :]
        cos, sin = jnp.cos(ang)[None, :, None, :], jnp.sin(ang)[None, :, None, :]
        t1, t2 = t[..., :quarter].astype(F32), t[..., quarter:].astype(F32)
        return jnp.concatenate([t1 * cos - t2 * sin, t1 * sin + t2 * cos], -1)

    return jnp.concatenate([rot(x[..., :half], row), rot(x[..., half:], col)], -1).astype(x.dtype)


def _softmax_attn(q, k, v):
    s = jnp.einsum('bqhd,bkhd->bhqk', q, k).astype(F32) * q.shape[-1] ** -0.5
    p = jax.nn.softmax(s, axis=-1).astype(v.dtype)
    return jnp.einsum('bhqk,bkhd->bqhd', p, v)


def _neigh_attn(q, k, v, k_ctx, v_ctx, rel_bias):
    b, l, h, dh = q.shape
    rows = l // GRID_W
    wr = min(WIN_R, rows)
    nj = GRID_W // QB_C
    band = QB_C + WIN_C
    n_lat = wr * band
    q = q * dh ** -0.5
    kg = k.reshape(b, rows, GRID_W, h, dh)
    vg = v.reshape(b, rows, GRID_W, h, dh)
    q_rows = jnp.moveaxis(q.reshape(b, rows, nj, QB_C, h, dh), 1, 0)
    r_idx = jnp.arange(rows)
    r0 = jnp.clip(r_idx - wr // 2, 0, rows - wr)
    dr_idx = r0[:, None] + jnp.arange(wr)[None, :] - r_idx[:, None] + (WIN_R - 1)
    j_idx = jnp.arange(nj)
    key_cols = jnp.clip(j_idx * QB_C - WIN_C // 2, 0, GRID_W - band)[:, None] + jnp.arange(band)
    q_cols = j_idx[:, None] * QB_C + jnp.arange(QB_C)
    c0 = jnp.clip(q_cols - WIN_C // 2, 0, GRID_W - WIN_C)
    kc = key_cols[:, None, :]
    col_ok = (kc >= c0[..., None]) & (kc < c0[..., None] + WIN_C)
    dc_idx = jnp.clip(kc - q_cols[..., None] + (WIN_C - 1), 0, 2 * WIN_C - 2)
    mask = jnp.broadcast_to(col_ok[:, :, None, :], (nj, QB_C, wr, band)).reshape(nj, QB_C, n_lat)

    def row_block(inp):
        q_r, r0_r, dr_r = inp

        def band_of(t):
            t = lax.dynamic_slice_in_dim(t, r0_r, wr, axis=1)
            t = jnp.take(t, key_cols, axis=2)
            return jnp.moveaxis(t, 2, 1).reshape(b, nj, n_lat, h, dh)

        k_r, v_r = band_of(kg), band_of(vg)
        bias = rel_bias[:, dr_r[None, None, :, None], dc_idx[:, :, None, :]].reshape(h, nj, QB_C, n_lat)
        s_lat = jnp.einsum('bjqhd,bjkhd->bhjqk', q_r, k_r).astype(F32) + bias.astype(F32)
        s_lat = jnp.where(mask, s_lat, -jnp.inf)
        s_ctx = jnp.einsum('bjqhd,bkhd->bhjqk', q_r, k_ctx).astype(F32)
        p = jax.nn.softmax(jnp.concatenate([s_lat, s_ctx], -1), axis=-1).astype(v.dtype)
        return (jnp.einsum('bhjqk,bjkhd->bjqhd', p[..., :n_lat], v_r)
                + jnp.einsum('bhjqk,bkhd->bjqhd', p[..., n_lat:], v_ctx))

    o = lax.map(row_block, (q_rows, r0, dr_idx))
    return jnp.moveaxis(o, 0, 1).reshape(b, l, h, dh)


def _mix_c(h_ctx, h_lat, w_in, rel_bias, w_out, need_ctx):
    b, l, _ = h_lat.shape
    q, k, v = (_heads(t, C_HEADS) for t in jnp.split(h_lat @ w_in, 3, axis=-1))
    pos = jnp.arange(l)
    row = (pos // GRID_W).astype(F32)
    col = (pos % GRID_W).astype(F32)
    q = _axial_rope(q, row, col)
    k = _axial_rope(k, row, col)
    if need_ctx:
        qc, kc, vc = (_heads(t, C_HEADS) for t in jnp.split(h_ctx @ w_in, 3, axis=-1))
    else:
        kc, vc = (_heads(t, C_HEADS) for t in jnp.split(h_ctx @ w_in[:, D_MODEL:], 2, axis=-1))
    y_lat = _neigh_attn(q, k, v, kc, vc, rel_bias).reshape(b, l, D_MODEL) @ w_out
    y_ctx = None
    if need_ctx:
        y_ctx = _softmax_attn(qc, kc, vc).reshape(b, h_ctx.shape[1], D_MODEL) @ w_out
    return y_ctx, y_lat


def _conv_ffn(h, w_up, w_dw, b_dw, w_down):
    a, gate = jnp.split(h @ w_up, 2, axis=-1)
    a = _dwconv(a, w_dw) + b_dw
    return (jax.nn.gelu(a, approximate=False) * gate) @ w_down


def setup_inputs(seed: int = 0) -> dict:
    key = jax.random.key(seed)
    ks = jax.random.split(key, 24)

    def nrm(k, shape, s):
        return jax.random.normal(k, shape, F32) * s

    dt = jnp.exp(jax.random.uniform(ks[13], (N_EVEN, 2, B_HEADS), F32, math.log(1e-3), math.log(1e-1)))
    return {
        'x': nrm(ks[0], (BATCH, SEQ, D_MODEL), 1.0),
        'c': nrm(ks[1], (BATCH, D_MODEL), 1.0),
        'ctx': nrm(ks[2], (BATCH, CTX_LEN, D_MODEL), 1.0),
        'c_ctx': nrm(ks[3], (D_MODEL,), 1.0),
        'w_ada': nrm(ks[4], (DEPTH, D_MODEL, 6 * D_MODEL), 0.5 * D_MODEL ** -0.5),
        'b_ada': nrm(ks[5], (DEPTH, 6 * D_MODEL), 0.01),
        'ln_g': 1.0 + nrm(ks[6], (DEPTH, 2, D_MODEL), 0.01),
        'ln_b': nrm(ks[7], (DEPTH, 2, D_MODEL), 0.01),
        'w_in_ab': nrm(ks[8], (N_EVEN, D_MODEL, N_IN_AB), D_MODEL ** -0.5),
        'hgrn_lb': nrm(ks[9], (N_EVEN, 2, A_KEY), 0.1),
        'hgrn_norm': 1.0 + nrm(ks[10], (N_EVEN, A_HEAD_V), 0.01),
        'gdn_conv': nrm(ks[11], (N_EVEN, B_CONV_W, 2 * B_KEY + B_VAL), B_CONV_W ** -0.5),
        'gdn_a_log': jnp.log(jax.random.uniform(ks[12], (N_EVEN, 2, B_HEADS), F32, 1.0, 16.0)),
        'gdn_dt_bias': dt + jnp.log(-jnp.expm1(-dt)),
        'gdn_norm': 1.0 + nrm(ks[14], (N_EVEN, B_HEAD_V), 0.01),
        'w_out_ab': nrm(ks[15], (N_EVEN, A_VAL + B_VAL, D_MODEL), BETA * (A_VAL + B_VAL) ** -0.5),
        'w_in_c': nrm(ks[16], (N_ODD, D_MODEL, 3 * D_MODEL), D_MODEL ** -0.5),
        'na_rel_bias': nrm(ks[17], (N_ODD, C_HEADS, 2 * WIN_R - 1, 2 * WIN_C - 1), 0.1),
        'w_out_c': nrm(ks[18], (N_ODD, D_MODEL, D_MODEL), BETA * D_MODEL ** -0.5),
        'ffn_w_up': nrm(ks[19], (DEPTH, D_MODEL, 2 * D_FF), D_MODEL ** -0.5),
        'ffn_w_dw': nrm(ks[20], (DEPTH, FFN_CONV_W, D_FF), FFN_CONV_W ** -0.5),
        'ffn_b_dw': nrm(ks[21], (DEPTH, D_FF), 0.01),
        'ffn_w_down': nrm(ks[22], (DEPTH, D_FF, D_MODEL), BETA * D_FF ** -0.5),
    }


def reference(x, c, ctx, c_ctx, w_ada, b_ada, ln_g, ln_b, w_in_ab, hgrn_lb, hgrn_norm, gdn_conv,
              gdn_a_log, gdn_dt_bias, gdn_norm, w_out_ab, w_in_c, na_rel_bias, w_out_c,
              ffn_w_up, ffn_w_dw, ffn_b_dw, ffn_w_down):
    lb_all = jax.nn.softmax(hgrn_lb.astype(F32), axis=0)
    lb_all = jnp.cumsum(lb_all, axis=0) - lb_all[0]
    s_lat = jax.nn.silu(c)
    s_ctx = jax.nn.silu(c_ctx)
    xl, xc = x, ctx
    for layer in range(DEPTH):
        need_ctx = layer < DEPTH - 1
        ml = jnp.split((s_lat @ w_ada[layer] + b_ada[layer])[:, None, :], 6, axis=-1)
        mc = jnp.split(s_ctx @ w_ada[layer] + b_ada[layer], 6, axis=-1)
        h_lat = xl * (1.0 + ml[1]) + ml[0]
        h_ctx = xc * (1.0 + mc[1]) + mc[0]
        if layer % 2 == 0:
            e = layer // 2
            y_ctx, y_lat = _mix_ab(h_ctx, h_lat, w_in_ab[e], lb_all[e], hgrn_norm[e], gdn_conv[e],
                                   gdn_a_log[e], gdn_dt_bias[e], gdn_norm[e], w_out_ab[e], need_ctx)
        else:
            o = layer // 2
            y_ctx, y_lat = _mix_c(h_ctx, h_lat, w_in_c[o], na_rel_bias[o], w_out_c[o], need_ctx)
        xl = _layernorm(ALPHA * xl + ml[2] * y_lat, ln_g[layer, 0], ln_b[layer, 0])
        f_lat = _conv_ffn(xl * (1.0 + ml[4]) + ml[3], ffn_w_up[layer], ffn_w_dw[layer],
                          ffn_b_dw[layer], ffn_w_down[layer])
        xl = _layernorm(ALPHA * xl + ml[5] * f_lat, ln_g[layer, 1], ln_b[layer, 1])
        if need_ctx:
            xc = _layernorm(ALPHA * xc + mc[2] * y_ctx, ln_g[layer, 0], ln_b[layer, 0])
            f_ctx = _conv_ffn(xc * (1.0 + mc[4]) + mc[3], ffn_w_up[layer], ffn_w_dw[layer],
                              ffn_b_dw[layer], ffn_w_down[layer])
            xc = _layernorm(ALPHA * xc + mc[5] * f_ctx, ln_g[layer, 1], ln_b[layer, 1])
    return xl
```

```python
import functools

import numpy as np
import jax
import jax.numpy as jnp
from jax import lax
from jax.experimental import pallas as pl
from jax.experimental.pallas import tpu as pltpu

F32 = jnp.float32
BF16 = jnp.bfloat16
HIGHEST = lax.Precision.HIGHEST

HEAD = 128
GRID_W = 64
WIN_R = 8
WIN_C = 16
ROPE_BASE = 10000.0
A_CHUNK = 32
B_CHUNK = 64
EPS = 1e-6
NEG = -1e30
HALO = 16
FF_ALIGN = 512
VMEM_LIMIT = 56 * 1024 * 1024


def _cparams(*sem):
    return pltpu.CompilerParams(dimension_semantics=sem, vmem_limit_bytes=VMEM_LIMIT)


def _pick(n, prefs):
    for p in prefs:
        if n % p == 0:
            return p
    return n


def _sigmoid(x):
    return 1.0 / (1.0 + jnp.exp(-x))


def _silu(x):
    return x * _sigmoid(x)


def _softplus(x):
    return jnp.maximum(x, 0.0) + jnp.log(1.0 + jnp.exp(-jnp.abs(x)))


def _dot(a, b):
    return jnp.dot(a.astype(BF16), b.astype(BF16), preferred_element_type=F32)


def _dot_nt(a, b):
    return lax.dot_general(a.astype(BF16), b.astype(BF16), (((1,), (1,)), ((), ())),
                           preferred_element_type=F32)


def _dot_tn(a, b):
    return lax.dot_general(a.astype(BF16), b.astype(BF16), (((0,), (0,)), ((), ())),
                           preferred_element_type=F32)


def _dot_hi(a, b):
    return jnp.dot(a, b, precision=HIGHEST, preferred_element_type=F32)


def _layernorm_rows(r, g, b):
    mu = jnp.mean(r, axis=-1, keepdims=True)
    rc = r - mu
    var = jnp.mean(rc * rc, axis=-1, keepdims=True)
    return rc * lax.rsqrt(var + EPS) * g + b


def _ada_kernel(c_ref, w_ref, b_ref, o_ref):
    s = _silu(c_ref[...])
    o_ref[0] = _dot(s, w_ref[0]) + b_ref[0]


def _ada(cc, w_ada, b_ada):
    depth, d, n = w_ada.shape
    rows = cc.shape[0]
    tn = _pick(n, (1024, 512, 256, 128))
    return pl.pallas_call(
        _ada_kernel,
        grid=(depth, n // tn),
        in_specs=[pl.BlockSpec((rows, d), lambda l, j: (0, 0)),
                  pl.BlockSpec((1, d, tn), lambda l, j: (l, 0, j)),
                  pl.BlockSpec((1, 1, tn), lambda l, j: (l, 0, j))],
        out_specs=pl.BlockSpec((1, rows, tn), lambda l, j: (l, 0, j)),
        out_shape=jax.ShapeDtypeStruct((depth, rows, n), F32),
        compiler_params=_cparams("parallel", "parallel"),
        name="ada_modulation",
    )(cc, w_ada, b_ada.reshape(depth, 1, n))


def _proj_kernel(x_ref, mod_ref, w_ref, *rest, shift, scale, has_gate):
    if has_gate:
        wg_ref, o_ref, og_ref, xs_ref = rest
    else:
        o_ref, xs_ref = rest

    @pl.when(pl.program_id(2) == 0)
    def _():
        h = x_ref[0] * (1.0 + mod_ref[0, scale:scale + 1, :]) + mod_ref[0, shift:shift + 1, :]
        xs_ref[...] = h.astype(BF16)
        if has_gate:
            og_ref[0] = jnp.dot(xs_ref[...], wg_ref[...], preferred_element_type=F32)

    o_ref[0] = jnp.dot(xs_ref[...], w_ref[...], preferred_element_type=F32).astype(o_ref.dtype)


def _proj(x, mod, w, wg=None, *, shift, scale, out_dtype=F32, name):
    g, r, k = x.shape
    n = w.shape[1]
    tm = _pick(r, (1024, 512, 256, 128, 64))
    tn = _pick(n, (1024, 512, 256, 128))
    has_gate = wg is not None
    gm = mod.shape[0]
    in_specs = [pl.BlockSpec((1, tm, k), lambda a, i, j: (a, i, 0)),
                pl.BlockSpec((1, 6, k), lambda a, i, j: (a % gm, 0, 0)),
                pl.BlockSpec((k, tn), lambda a, i, j: (0, j))]
    out_specs = [pl.BlockSpec((1, tm, tn), lambda a, i, j: (a, i, j))]
    out_shape = [jax.ShapeDtypeStruct((g, r, n), out_dtype)]
    args = [x, mod, w]
    if has_gate:
        in_specs.append(pl.BlockSpec((k, HEAD), lambda a, i, j: (0, 0)))
        out_specs.append(pl.BlockSpec((1, tm, HEAD), lambda a, i, j: (a, i, 0)))
        out_shape.append(jax.ShapeDtypeStruct((g, r, HEAD), F32))
        args.append(wg)
    outs = pl.pallas_call(
        functools.partial(_proj_kernel, shift=shift, scale=scale, has_gate=has_gate),
        grid=(g, r // tm, n // tn),
        in_specs=in_specs, out_specs=out_specs, out_shape=out_shape,
        scratch_shapes=[pltpu.VMEM((tm, k), BF16)],
        compiler_params=_cparams("parallel", "parallel", "arbitrary"),
        name=name,
    )(*args)
    return outs if has_gate else outs[0]


def _merge_half(of_ref, ob_ref, gate_ref, norm_ref, ys_ref, col0, nheads):
    for h in range(nheads):
        hs = slice(h * HEAD, (h + 1) * HEAD)
        o = of_ref[0, :, hs] + ob_ref[0, :, hs]
        ms = jnp.mean(o * o, axis=-1, keepdims=True)
        y = o * lax.rsqrt(ms + EPS) * norm_ref[...] * _silu(gate_ref[0, :, hs])
        ys_ref[:, col0 + h * HEAD:col0 + (h + 1) * HEAD] = y.astype(BF16)


def _outproj_kernel(*refs, merge, alpha, a_heads, b_heads):
    if merge:
        (oaf, oab, obf, obb, ga, zb, an, bn, w_ref, res_ref, mod_ref, g_ref, b_ref, o_ref, ys_ref) = refs
        _merge_half(oaf, oab, ga, an, ys_ref, 0, a_heads)
        _merge_half(obf, obb, zb, bn, ys_ref, a_heads * HEAD, b_heads)
        y = ys_ref[...]
    else:
        x_ref, w_ref, res_ref, mod_ref, g_ref, b_ref, o_ref = refs
        y = x_ref[0]
    acc = jnp.dot(y, w_ref[...], preferred_element_type=F32)
    r = alpha * res_ref[0] + mod_ref[0, 2:3, :] * acc
    o_ref[0] = _layernorm_rows(r, g_ref[...], b_ref[...])


def _outproj(xs, w, res, mod, ln_g, ln_b, *, alpha, merge=None, name):
    g, r, d = res.shape
    k = w.shape[0]
    tm = _pick(r, (256, 128, 64))
    gm = mod.shape[0]
    row = lambda a, i: (a, i, 0)
    fixed = lambda a, i: (0, 0)
    if merge is None:
        in_specs = [pl.BlockSpec((1, tm, k), row)]
        args = [xs]
        scratch = []
        kern = functools.partial(_outproj_kernel, merge=False, alpha=alpha, a_heads=0, b_heads=0)
    else:
        oaf, oab, obf, obb, p, an, bn = xs
        a_val, b_val, ga_blk, zb_blk = merge
        in_specs = [pl.BlockSpec((1, tm, a_val), row), pl.BlockSpec((1, tm, a_val), row),
                    pl.BlockSpec((1, tm, b_val), row), pl.BlockSpec((1, tm, b_val), row),
                    pl.BlockSpec((1, tm, a_val), lambda a, i: (a, i, ga_blk)),
                    pl.BlockSpec((1, tm, b_val), lambda a, i: (a, i, zb_blk)),
                    pl.BlockSpec((1, HEAD), fixed), pl.BlockSpec((1, HEAD), fixed)]
        args = [oaf, oab, obf, obb, p, p, an, bn]
        scratch = [pltpu.VMEM((tm, k), BF16)]
        kern = functools.partial(_outproj_kernel, merge=True, alpha=alpha,
                                 a_heads=a_val // HEAD, b_heads=b_val // HEAD)
    in_specs += [pl.BlockSpec((k, d), fixed),
                 pl.BlockSpec((1, tm, d), row),
                 pl.BlockSpec((1, 6, d), lambda a, i: (a % gm, 0, 0)),
                 pl.BlockSpec((1, d), fixed), pl.BlockSpec((1, d), fixed)]
    args += [w, res, mod, ln_g.reshape(1, d), ln_b.reshape(1, d)]
    return pl.pallas_call(
        kern,
        grid=(g, r // tm),
        in_specs=in_specs,
        out_specs=pl.BlockSpec((1, tm, d), row),
        out_shape=jax.ShapeDtypeStruct((g, r, d), F32),
        scratch_shapes=scratch,
        compiler_params=_cparams("parallel", "parallel"),
        name=name,
    )(*args)


def _tri_masks(n):
    row = lax.broadcasted_iota(jnp.int32, (n, n), 0)
    col = lax.broadcasted_iota(jnp.int32, (n, n), 1)
    return col <= row, col >= row


def _gla_kernel(qf_ref, zf_ref, vf_ref, qb_ref, zb_ref, vb_ref, lb_ref, s0_ref,
                of_ref, ob_ref, sfin_ref, st_ref, *, hb, nchunk):
    t = pl.program_id(2)
    c = A_CHUNK

    @pl.when(t == 0)
    def _():
        st_ref[...] = s0_ref[0]

    causal, anti = _tri_masks(c)
    streams = ((qf_ref, zf_ref, vf_ref, of_ref, causal, c // 2 - 1, c - 1),
               (qb_ref, zb_ref, vb_ref, ob_ref, anti, c - c // 2, 0))

    def body(ci, carry):
        for d, (q_ref, z_ref, v_ref, o_ref, mask, mid, last) in enumerate(streams):
            cc = ci if d == 0 else nchunk - 1 - ci
            sl = pl.ds(pl.multiple_of(cc * c, c), c)
            q = _silu(q_ref[0, sl, :])
            v = v_ref[0, sl, :]
            lb = lb_ref[d:d + 1, :]
            f = lb + (1.0 - lb) * _sigmoid(z_ref[0, sl, :])
            k = 1.0 - f
            cum = _dot_hi(mask.astype(F32), jnp.log(f))
            m = cum[mid:mid + 1, :]
            bl = cum[last:last + 1, :]
            qe = q * jnp.exp(cum)
            qm = q * jnp.exp(cum - m)
            km = k * jnp.exp(m - cum)
            kd = k * jnp.exp(bl - cum)
            dec = jnp.exp(bl)
            for h in range(hb):
                hs = slice(h * HEAD, (h + 1) * HEAD)
                a = jnp.where(mask, _dot_nt(qm[:, hs], km[:, hs]), 0.0)
                st = st_ref[d, h]
                o_ref[0, sl, hs] = _dot_nt(qe[:, hs], st) + _dot(a, v[:, hs])
                st_ref[d, h] = st * dec[:, hs] + _dot_tn(v[:, hs], kd[:, hs])
        return carry

    lax.fori_loop(0, nchunk, body, 0)

    @pl.when(t == pl.num_programs(2) - 1)
    def _():
        sfin_ref[0] = st_ref[...]


def _gla(p, lb, s0, *, a_key, name):
    b, l, _ = p.shape
    heads = a_key // HEAD
    hb = _pick(heads, (4, 2, 1))
    wb = hb * HEAD
    per = a_key // wb
    tb = _pick(l, (256, 128, 64, 32))
    nblk = l // tb
    fwd = lambda grp: (lambda bi, hg, t: (bi, t, grp * per + hg))
    bwd = lambda grp: (lambda bi, hg, t: (bi, nblk - 1 - t, grp * per + hg))
    blk = (1, tb, wb)
    st_blk = (1, 2, hb, HEAD, HEAD)
    st_map = lambda bi, hg, t: (bi, 0, hg, 0, 0)
    return pl.pallas_call(
        functools.partial(_gla_kernel, hb=hb, nchunk=tb // A_CHUNK),
        grid=(b, heads // hb, nblk),
        in_specs=[pl.BlockSpec(blk, fwd(0)), pl.BlockSpec(blk, fwd(1)), pl.BlockSpec(blk, fwd(3)),
                  pl.BlockSpec(blk, bwd(0)), pl.BlockSpec(blk, bwd(2)), pl.BlockSpec(blk, bwd(3)),
                  pl.BlockSpec((2, wb), lambda bi, hg, t: (0, hg)),
                  pl.BlockSpec(st_blk, st_map)],
        out_specs=[pl.BlockSpec(blk, lambda bi, hg, t: (bi, t, hg)),
                   pl.BlockSpec(blk, lambda bi, hg, t: (bi, nblk - 1 - t, hg)),
                   pl.BlockSpec(st_blk, st_map)],
        out_shape=[jax.ShapeDtypeStruct((b, l, a_key), F32),
                   jax.ShapeDtypeStruct((b, l, a_key), F32),
                   jax.ShapeDtypeStruct(s0.shape, F32)],
        scratch_shapes=[pltpu.VMEM((2, hb, HEAD, HEAD), F32)],
        compiler_params=_cparams("parallel", "parallel", "arbitrary"),
        name=name,
    )(p, p, p, p, p, p, lb, s0)


def _gdn_prep_kernel(x_ref, w_ref, o_ref, *, qk_blocks, q_blocks, width):
    x = x_ref[0]
    l = x.shape[0]
    pos = lax.broadcasted_iota(jnp.int32, x.shape, 0)
    pad = width // 2
    acc = x * w_ref[pad:pad + 1, :]
    for j in range(width):
        s = j - pad
        if s == 0:
            continue
        shifted = pltpu.roll(x, (-s) % l, 0)
        ok = (pos + s >= 0) & (pos + s < l)
        acc = acc + jnp.where(ok, shifted, 0.0) * w_ref[j:j + 1, :]
    y = _silu(acc)
    cb = pl.program_id(1)
    inv = lax.rsqrt(jnp.sum(y * y, axis=-1, keepdims=True) + EPS)
    inv = jnp.where(cb < qk_blocks, inv, 1.0) * jnp.where(cb < q_blocks, HEAD ** -0.5, 1.0)
    o_ref[0] = y * inv


def _gdn_prep(p, conv_w, *, col0, b_key, name):
    b, l, _ = p.shape
    width, c = conv_w.shape
    nb = c // HEAD
    off = col0 // HEAD
    return pl.pallas_call(
        functools.partial(_gdn_prep_kernel, qk_blocks=2 * b_key // HEAD, q_blocks=b_key // HEAD, width=width),
        grid=(b, nb),
        in_specs=[pl.BlockSpec((1, l, HEAD), lambda bi, cb: (bi, 0, off + cb)),
                  pl.BlockSpec((width, HEAD), lambda bi, cb: (0, cb))],
        out_specs=pl.BlockSpec((1, l, HEAD), lambda bi, cb: (bi, 0, cb)),
        out_shape=jax.ShapeDtypeStruct((b, l, c), F32),
        compiler_params=_cparams("parallel", "parallel"),
        name=name,
    )(p, conv_w)


def _unit_lower_inverse(lm, eye):
    n = lm.shape[0]
    nm = -lm
    p = eye + nm
    steps = int(np.ceil(np.log2(n))) - 1
    for _ in range(steps):
        nm = _dot_hi(nm, nm)
        p = p + _dot_hi(p, nm)
    return p


def _gdn_kernel(qf_ref, kf_ref, vf_ref, gcf_ref, grf_ref, qb_ref, kb_ref, vb_ref, gcb_ref, grb_ref,
                pc_ref, pr_ref, s0_ref, of_ref, ob_ref, sfin_ref, st_ref, *, hb, heads, nchunk):
    t = pl.program_id(2)
    hg = pl.program_id(1)
    c = B_CHUNK

    @pl.when(t == 0)
    def _():
        st_ref[...] = s0_ref[0]

    causal, anti = _tri_masks(c)
    row = lax.broadcasted_iota(jnp.int32, (c, c), 0)
    col = lax.broadcasted_iota(jnp.int32, (c, c), 1)
    eye = (row == col).astype(F32)
    streams = ((qf_ref, kf_ref, vf_ref, gcf_ref, grf_ref, of_ref, causal, col < row, c - 1),
               (qb_ref, kb_ref, vb_ref, gcb_ref, grb_ref, ob_ref, anti, col > row, 0))

    def body(ci, carry):
        for d, (q_ref, k_ref, v_ref, gc_ref, gr_ref, o_ref, incl, strict, last) in enumerate(streams):
            cc = ci if d == 0 else nchunk - 1 - ci
            sl = pl.ds(pl.multiple_of(cc * c, c), c)
            gcol = gc_ref[0, sl, :]
            a_col = gcol[:, d * heads:(d + 1) * heads]
            beta_col = _sigmoid(gcol[:, (2 + d) * heads:(3 + d) * heads])
            a_log = pr_ref[d:d + 1, 0:heads]
            dt = pr_ref[2 + d:3 + d, 0:heads]
            g_col = _dot_hi(incl.astype(F32), -jnp.exp(a_log) * _softplus(a_col + dt))
            grow = gr_ref[0, cc]
            a_row = grow[d * heads:(d + 1) * heads, :]
            a_log_c = pc_ref[0:heads, d:d + 1]
            dt_c = pc_ref[0:heads, 2 + d:3 + d]
            g_row = _dot_hi(-jnp.exp(a_log_c) * _softplus(a_row + dt_c),
                            (anti if d == 0 else causal).astype(F32))
            for hl in range(hb):
                hh = hg * hb + hl
                hs = slice(hl * HEAD, (hl + 1) * HEAD)
                onehot_c = (lax.broadcasted_iota(jnp.int32, (1, heads), 1) == hh).astype(F32)
                onehot_r = (lax.broadcasted_iota(jnp.int32, (heads, 1), 0) == hh).astype(F32)
                gc = jnp.sum(g_col * onehot_c, axis=1, keepdims=True)
                bc = jnp.sum(beta_col * onehot_c, axis=1, keepdims=True)
                gr = jnp.sum(g_row * onehot_r, axis=0, keepdims=True)
                q = q_ref[0, sl, hs]
                k = k_ref[0, sl, hs]
                v = v_ref[0, sl, hs]
                decay = jnp.exp(jnp.where(incl, gc - gr, NEG))
                kb = k * bc
                lm = jnp.where(strict, _dot_nt(kb, k) * decay, 0.0)
                rhs = jnp.concatenate([v * bc, kb * jnp.exp(gc)], axis=1)
                sol = _dot_hi(_unit_lower_inverse(lm, eye), rhs)
                u = sol[:, :HEAD]
                w = sol[:, HEAD:]
                s = st_ref[d, hl]
                v_new = u - _dot(w, s)
                attn = _dot_nt(q, k) * decay
                o_ref[0, sl, hs] = _dot(q * jnp.exp(gc), s) + _dot(attn, v_new)
                g_last = gc[last:last + 1, :]
                st_ref[d, hl] = s * jnp.exp(g_last) + _dot_tn(k * jnp.exp(g_last - gc), v_new)
        return carry

    lax.fori_loop(0, nchunk, body, 0)

    @pl.when(t == pl.num_programs(2) - 1)
    def _():
        sfin_ref[0] = st_ref[...]


def _gdn(qkv, gcol, grow, prow, pcol, s0, *, b_key, name):
    b, l, _ = qkv.shape
    heads = b_key // HEAD
    hb = _pick(heads, (4, 2, 1))
    wb = hb * HEAD
    per = b_key // wb
    tb = _pick(l, (256, 128, 64))
    nblk = l // tb
    cpb = tb // B_CHUNK
    fwd = lambda grp: (lambda bi, hg, t: (bi, t, grp * per + hg))
    bwd = lambda grp: (lambda bi, hg, t: (bi, nblk - 1 - t, grp * per + hg))
    blk = (1, tb, wb)
    gc_blk = (1, tb, HEAD)
    gr_blk = (1, cpb, 4 * heads, B_CHUNK)
    st_blk = (1, 2, hb, HEAD, HEAD)
    st_map = lambda bi, hg, t: (bi, 0, hg, 0, 0)
    return pl.pallas_call(
        functools.partial(_gdn_kernel, hb=hb, heads=heads, nchunk=cpb),
        grid=(b, heads // hb, nblk),
        in_specs=[pl.BlockSpec(blk, fwd(0)), pl.BlockSpec(blk, fwd(1)), pl.BlockSpec(blk, fwd(2)),
                  pl.BlockSpec(gc_blk, lambda bi, hg, t: (bi, t, 0)),
                  pl.BlockSpec(gr_blk, lambda bi, hg, t: (bi, t, 0, 0)),
                  pl.BlockSpec(blk, bwd(0)), pl.BlockSpec(blk, bwd(1)), pl.BlockSpec(blk, bwd(2)),
                  pl.BlockSpec(gc_blk, lambda bi, hg, t: (bi, nblk - 1 - t, 0)),
                  pl.BlockSpec(gr_blk, lambda bi, hg, t: (bi, nblk - 1 - t, 0, 0)),
                  pl.BlockSpec(pcol.shape, lambda bi, hg, t: (0, 0)),
                  pl.BlockSpec(prow.shape, lambda bi, hg, t: (0, 0)),
                  pl.BlockSpec(st_blk, st_map)],
        out_specs=[pl.BlockSpec(blk, lambda bi, hg, t: (bi, t, hg)),
                   pl.BlockSpec(blk, lambda bi, hg, t: (bi, nblk - 1 - t, hg)),
                   pl.BlockSpec(st_blk, st_map)],
        out_shape=[jax.ShapeDtypeStruct((b, l, b_key), F32),
                   jax.ShapeDtypeStruct((b, l, b_key), F32),
                   jax.ShapeDtypeStruct(s0.shape, F32)],
        scratch_shapes=[pltpu.VMEM((2, hb, HEAD, HEAD), F32)],
        compiler_params=_cparams("parallel", "parallel", "arbitrary"),
        name=name,
    )(qkv, qkv, qkv, gcol, grow, qkv, qkv, qkv, gcol, grow, pcol, prow, s0)


def _rope(x, cos, s1, s2):
    q = HEAD // 4
    return x * cos + pltpu.roll(x, HEAD - q, 1) * s1 + pltpu.roll(x, q, 1) * s2


def _rope_tables(l):
    quarter = HEAD // 4
    pos = jnp.arange(l)
    inv = ROPE_BASE ** (-jnp.arange(quarter, dtype=F32) / quarter)
    zero = jnp.zeros((l, quarter), F32)
    cos, s1, s2 = [], [], []
    for p in ((pos // GRID_W).astype(F32), (pos % GRID_W).astype(F32)):
        ang = p[:, None] * inv[None, :]
        cos += [jnp.cos(ang), jnp.cos(ang)]
        s1 += [-jnp.sin(ang), zero]
        s2 += [zero, jnp.sin(ang)]
    return (jnp.concatenate(cos, -1), jnp.concatenate(s1, -1), jnp.concatenate(s2, -1))


def _nattn_plan(rows):
    wr = min(WIN_R, rows)
    tr = min(4, rows)
    kbr = min(rows, tr + wr)
    qc = np.arange(GRID_W)
    c0 = np.clip(qc - WIN_C // 2, 0, GRID_W - WIN_C)
    ok_c = (qc[None, :] >= c0[:, None]) & (qc[None, :] < c0[:, None] + WIN_C)
    dc = np.clip(qc[None, :] - qc[:, None] + WIN_C - 1, 0, 2 * WIN_C - 2)
    tiles, classes, keys = [], [], {}
    for t in range(rows // tr):
        ks = int(np.clip(tr * t - wr // 2, 0, rows - kbr))
        qr = tr * t + np.arange(tr)
        kr = ks + np.arange(kbr)
        r0 = np.clip(qr - wr // 2, 0, rows - wr)
        ok_r = (kr[None, :] >= r0[:, None]) & (kr[None, :] < r0[:, None] + wr)
        dr = np.clip(kr[None, :] - qr[:, None] + WIN_R - 1, 0, 2 * WIN_R - 2)
        valid = ok_r[:, None, :, None] & ok_c[None, :, None, :]
        idx = dr[:, None, :, None] * (2 * WIN_C - 1) + dc[None, :, None, :]
        idx = np.where(valid, idx, 0).reshape(tr * GRID_W, kbr * GRID_W)
        valid = valid.reshape(tr * GRID_W, kbr * GRID_W)
        key = idx.tobytes() + valid.tobytes()
        if key not in keys:
            keys[key] = len(classes)
            classes.append((idx, valid))
        tiles.append((tr * t * GRID_W, ks * GRID_W, keys[key]))
    idx = np.stack([cl[0] for cl in classes]).astype(np.int32)
    valid = np.stack([cl[1] for cl in classes])
    return tuple(tiles), idx, valid


def _nattn_kernel(q_ref, k_ref, v_ref, kc_ref, vc_ref, bias_ref, cos_ref, s1_ref, s2_ref,
                  o_ref, kr_ref, vr_ref, *, tiles):
    tq, tk = bias_ref.shape[2], bias_ref.shape[3]
    kr_ref[...] = _rope(k_ref[0], cos_ref[...], s1_ref[...], s2_ref[...]).astype(BF16)
    vr_ref[...] = v_ref[0].astype(BF16)
    kc = kc_ref[0].astype(BF16)
    vc = vc_ref[0].astype(BF16)
    for q0, k0, cls in tiles:
        qs = slice(q0, q0 + tq)
        qt = _rope(q_ref[0, qs, :], cos_ref[qs, :], s1_ref[qs, :], s2_ref[qs, :]) * HEAD ** -0.5
        qt = qt.astype(BF16)
        s_lat = _dot_nt(qt, kr_ref[k0:k0 + tk, :]) + bias_ref[0, cls]
        s_ctx = _dot_nt(qt, kc)
        m = jnp.maximum(jnp.max(s_lat, axis=-1, keepdims=True), jnp.max(s_ctx, axis=-1, keepdims=True))
        p_lat = jnp.exp(s_lat - m)
        p_ctx = jnp.exp(s_ctx - m)
        den = jnp.sum(p_lat, axis=-1, keepdims=True) + jnp.sum(p_ctx, axis=-1, keepdims=True)
        o = _dot(p_lat, vr_ref[k0:k0 + tk, :]) + _dot(p_ctx, vc)
        o_ref[0, qs, :] = (o / den).astype(o_ref.dtype)


def _nattn(p_lat, p_ctx, bias, tiles, ropes, *, heads, kc_blk, vc_blk, name):
    b, l, _ = p_lat.shape
    lc = p_ctx.shape[1]
    ncls, tq, tk = bias.shape[1:]
    tab = pl.BlockSpec((l, HEAD), lambda h, bi: (0, 0))
    return pl.pallas_call(
        functools.partial(_nattn_kernel, tiles=tiles),
        grid=(heads, b),
        in_specs=[pl.BlockSpec((1, l, HEAD), lambda h, bi: (bi, 0, h)),
                  pl.BlockSpec((1, l, HEAD), lambda h, bi: (bi, 0, heads + h)),
                  pl.BlockSpec((1, l, HEAD), lambda h, bi: (bi, 0, 2 * heads + h)),
                  pl.BlockSpec((1, lc, HEAD), lambda h, bi: (bi, 0, kc_blk + h)),
                  pl.BlockSpec((1, lc, HEAD), lambda h, bi: (bi, 0, vc_blk + h)),
                  pl.BlockSpec((1, ncls, tq, tk), lambda h, bi: (h, 0, 0, 0)),
                  tab, tab, tab],
        out_specs=pl.BlockSpec((1, l, HEAD), lambda h, bi: (bi, 0, h)),
        out_shape=jax.ShapeDtypeStruct((b, l, heads * HEAD), BF16),
        scratch_shapes=[pltpu.VMEM((l, HEAD), BF16), pltpu.VMEM((l, HEAD), BF16)],
        compiler_params=_cparams("parallel", "parallel"),
        name=name,
    )(p_lat, p_lat, p_lat, p_ctx, p_ctx, bias, *ropes)


def _cattn_kernel(q_ref, k_ref, v_ref, o_ref):
    s = _dot_nt(q_ref[0] * HEAD ** -0.5, k_ref[0])
    p = jnp.exp(s - jnp.max(s, axis=-1, keepdims=True))
    o = _dot(p, v_ref[0]) / jnp.sum(p, axis=-1, keepdims=True)
    o_ref[0] = o.astype(o_ref.dtype)


def _cattn(p_ctx, *, heads, name):
    b, lc, _ = p_ctx.shape
    blk = (1, lc, HEAD)
    return pl.pallas_call(
        _cattn_kernel,
        grid=(b, heads),
        in_specs=[pl.BlockSpec(blk, lambda bi, h: (bi, 0, h)),
                  pl.BlockSpec(blk, lambda bi, h: (bi, 0, heads + h)),
                  pl.BlockSpec(blk, lambda bi, h: (bi, 0, 2 * heads + h))],
        out_specs=pl.BlockSpec(blk, lambda bi, h: (bi, 0, h)),
        out_shape=jax.ShapeDtypeStruct((b, lc, heads * HEAD), BF16),
        compiler_params=_cparams("parallel", "parallel"),
        name=name,
    )(p_ctx, p_ctx, p_ctx)


def _ffn_up_kernel(x_ref, xp_ref, xn_ref, mod_ref, wa_ref, wg_ref, wdw_ref, bdw_ref, o_ref, xs_ref):
    tm = x_ref.shape[1]
    i = pl.program_id(1)

    @pl.when(pl.program_id(2) == 0)
    def _():
        sc = 1.0 + mod_ref[0, 4:5, :]
        sh = mod_ref[0, 3:4, :]
        xs_ref[0:HALO, :] = (xp_ref[0] * sc + sh).astype(BF16)
        xs_ref[HALO:HALO + tm, :] = (x_ref[0] * sc + sh).astype(BF16)
        xs_ref[HALO + tm:, :] = (xn_ref[0] * sc + sh).astype(BF16)

    a = jnp.dot(xs_ref[...], wa_ref[...], preferred_element_type=F32)
    gate = jnp.dot(xs_ref[HALO:HALO + tm, :], wg_ref[...], preferred_element_type=F32)
    ext = tm + 2 * HALO
    a_prev = pltpu.roll(a, 1, 0)[HALO:HALO + tm]
    a_next = pltpu.roll(a, ext - 1, 0)[HALO:HALO + tm]
    pos = lax.broadcasted_iota(jnp.int32, gate.shape, 0)
    a_prev = jnp.where((pos == 0) & (i == 0), 0.0, a_prev)
    a_next = jnp.where((pos == tm - 1) & (i == pl.num_programs(1) - 1), 0.0, a_next)
    conv = (a_prev * wdw_ref[0:1, :] + a[HALO:HALO + tm] * wdw_ref[1:2, :] + a_next * wdw_ref[2:3, :]
            + bdw_ref[...])
    gelu = 0.5 * conv * (1.0 + lax.erf(conv * np.float32(np.sqrt(0.5))))
    o_ref[0] = (gelu * gate).astype(o_ref.dtype)


def _ffn_up(x, mod, wa, wg, wdw, bdw, *, name):
    g, r, k = x.shape
    f = wa.shape[1]
    tm = _pick(r, (1024, 512, 256, 128, 64, 32, 16))
    tn = _pick(f, (512, 256, 128))
    hpt = tm // HALO
    nh = r // HALO
    gm = mod.shape[0]
    return pl.pallas_call(
        _ffn_up_kernel,
        grid=(g, r // tm, f // tn),
        in_specs=[pl.BlockSpec((1, tm, k), lambda a, i, j: (a, i, 0)),
                  pl.BlockSpec((1, HALO, k), lambda a, i, j: (a, jnp.maximum(i * hpt - 1, 0), 0)),
                  pl.BlockSpec((1, HALO, k), lambda a, i, j: (a, jnp.minimum((i + 1) * hpt, nh - 1), 0)),
                  pl.BlockSpec((1, 6, k), lambda a, i, j: (a % gm, 0, 0)),
                  pl.BlockSpec((k, tn), lambda a, i, j: (0, j)),
                  pl.BlockSpec((k, tn), lambda a, i, j: (0, j)),
                  pl.BlockSpec((3, tn), lambda a, i, j: (0, j)),
                  pl.BlockSpec((1, tn), lambda a, i, j: (0, j))],
        out_specs=pl.BlockSpec((1, tm, tn), lambda a, i, j: (a, i, j)),
        out_shape=jax.ShapeDtypeStruct((g, r, f), BF16),
        scratch_shapes=[pltpu.VMEM((tm + 2 * HALO, k), BF16)],
        compiler_params=_cparams("parallel", "parallel", "arbitrary"),
        name=name,
    )(x, x, x, mod, wa, wg, wdw, bdw)


def _ffn_down_kernel(x_ref, w_ref, res_ref, mod_ref, g_ref, b_ref, o_ref, acc_ref, *, alpha):
    kk = pl.program_id(2)

    @pl.when(kk == 0)
    def _():
        acc_ref[...] = jnp.zeros_like(acc_ref)

    acc_ref[...] += jnp.dot(x_ref[0], w_ref[...], preferred_element_type=F32)

    @pl.when(kk == pl.num_programs(2) - 1)
    def _():
        r = alpha * res_ref[0] + mod_ref[0, 5:6, :] * acc_ref[...]
        o_ref[0] = _layernorm_rows(r, g_ref[...], b_ref[...])


def _ffn_down(x, w, res, mod, ln_g, ln_b, *, alpha, name):
    g, r, f = x.shape
    d = w.shape[1]
    tm = _pick(r, (512, 256, 128, 64))
    tk = _pick(f, (512, 256, 128))
    gm = mod.shape[0]
    row = lambda a, i, kk: (a, i, 0)
    fixed = lambda a, i, kk: (0, 0)
    return pl.pallas_call(
        functools.partial(_ffn_down_kernel, alpha=alpha),
        grid=(g, r // tm, f // tk),
        in_specs=[pl.BlockSpec((1, tm, tk), lambda a, i, kk: (a, i, kk)),
                  pl.BlockSpec((tk, d), lambda a, i, kk: (kk, 0)),
                  pl.BlockSpec((1, tm, d), row),
                  pl.BlockSpec((1, 6, d), lambda a, i, kk: (a % gm, 0, 0)),
                  pl.BlockSpec((1, d), fixed), pl.BlockSpec((1, d), fixed)],
        out_specs=pl.BlockSpec((1, tm, d), row),
        out_shape=jax.ShapeDtypeStruct((g, r, d), F32),
        scratch_shapes=[pltpu.VMEM((tm, d), F32)],
        compiler_params=_cparams("parallel", "parallel", "arbitrary"),
        name=name,
    )(x, w, res, mod, ln_g.reshape(1, d), ln_b.reshape(1, d))


def _pad_cols(w, n):
    return jnp.pad(w, ((0, 0), (0, n - w.shape[1])))


def kernel(x, c, ctx, c_ctx, w_ada, b_ada, ln_g, ln_b, w_in_ab, hgrn_lb, hgrn_norm, gdn_conv, gdn_a_log,
           gdn_dt_bias, gdn_norm, w_out_ab, w_in_c, na_rel_bias, w_out_c, ffn_w_up, ffn_w_dw, ffn_b_dw,
           ffn_w_down):
    b, l, d = x.shape
    lc = ctx.shape[1]
    depth = w_ada.shape[0]
    alpha = (2 * depth) ** 0.25
    a_key = hgrn_lb.shape[-1]
    b_heads = gdn_a_log.shape[-1]
    b_key = b_heads * HEAD
    b_val = gdn_conv.shape[-1] - 2 * b_key
    a_val = w_out_ab.shape[1] - b_val
    n_main = 3 * a_key + 2 * a_val + 2 * b_key + 2 * b_val
    c_heads = d // HEAD
    d_ff = ffn_w_down.shape[1]
    ffp = -(-d_ff // FF_ALIGN) * FF_ALIGN
    assert a_key == a_val == b_key == b_val and w_in_ab.shape[-1] == n_main + 4 * b_heads
    assert l % (4 * GRID_W) == 0 or l // GRID_W < 4

    rows = -(-(b + 1) // 8) * 8
    cc = jnp.concatenate([c, c_ctx[None, :], jnp.zeros((rows - b - 1, d), F32)], axis=0)
    mod = _ada(cc, w_ada, b_ada).reshape(depth, rows, 6, d)

    lb_all = jax.nn.softmax(hgrn_lb.astype(F32), axis=0)
    lb_all = jnp.cumsum(lb_all, axis=0) - lb_all[0]

    tiles, bias_idx, bias_ok = _nattn_plan(l // GRID_W)
    ropes = _rope_tables(l)
    s_zero = jnp.zeros((b, 2, a_key // HEAD, HEAD, HEAD), F32)

    xl, xc = x, ctx
    for layer in range(depth):
        need_ctx = layer < depth - 1
        mod_l = mod[layer, :b]
        mod_c = mod[layer, b:b + 1]
        g0, b0 = ln_g[layer, 0], ln_b[layer, 0]
        if layer % 2 == 0:
            e = layer // 2
            w_main = w_in_ab[e, :, :n_main].astype(BF16)
            w_gate = _pad_cols(w_in_ab[e, :, n_main:], HEAD).astype(BF16)
            w_out = w_out_ab[e].astype(BF16)
            an = hgrn_norm[e].reshape(1, HEAD)
            bn = gdn_norm[e].reshape(1, HEAD)
            prow = _pad_cols(jnp.concatenate([gdn_a_log[e], gdn_dt_bias[e]], axis=0), HEAD)
            pcol = _pad_cols(jnp.concatenate([gdn_a_log[e], gdn_dt_bias[e]], axis=0).T, HEAD)
            merge = (a_val, b_val, (3 * a_key + a_val) // a_val, (n_main - b_val) // b_val)
            sa, sb = s_zero, s_zero
            for is_ctx, xs, mods in ((True, xc, mod_c), (False, xl, mod_l)):
                tag = f"l{layer}_{'ctx' if is_ctx else 'lat'}"
                p, gates = _proj(xs, mods, w_main, w_gate, shift=0, scale=1,
                                 name=f"inproj_ab_{tag}")
                oaf, oab, sa = _gla(p, lb_all[e], sa, a_key=a_key, name=f"hgrn2_{tag}")
                qkv = _gdn_prep(p, gdn_conv[e], col0=3 * a_key + 2 * a_val, b_key=b_key, name=f"gdn_prep_{tag}")
                ls = xs.shape[1]
                grow = gates[:, :, :4 * b_heads].reshape(b, ls // B_CHUNK, B_CHUNK, 4 * b_heads)
                grow = jnp.swapaxes(grow, 2, 3)
                obf, obb, sb = _gdn(qkv, gates, grow, prow, pcol, sb, b_key=b_key, name=f"gdn_{tag}")
                if is_ctx and not need_ctx:
                    continue
                y = _outproj((oaf, oab, obf, obb, p, an, bn), w_out, xs, mods,
                             g0, b0, alpha=alpha, merge=merge, name=f"outproj_ab_{tag}")
                if is_ctx:
                    xc = y
                else:
                    xl = y
        else:
            o = layer // 2
            w_in = w_in_c[o].astype(BF16)
            w_out = w_out_c[o].astype(BF16)
            bias = jnp.where(bias_ok[None], na_rel_bias[o].reshape(c_heads, -1)[:, bias_idx], NEG)
            p_lat = _proj(xl, mod_l, w_in, shift=0, scale=1, name=f"inproj_c_l{layer}_lat")
            if need_ctx:
                p_ctx = _proj(xc, mod_c, w_in, shift=0, scale=1, name=f"inproj_c_l{layer}_ctx")
                kc_blk, vc_blk = c_heads, 2 * c_heads
            else:
                p_ctx = _proj(xc, mod_c, w_in[:, d:], shift=0, scale=1, name=f"inproj_c_l{layer}_ctx")
                kc_blk, vc_blk = 0, c_heads
            o_lat = _nattn(p_lat, p_ctx, bias, tiles, ropes, heads=c_heads, kc_blk=kc_blk, vc_blk=vc_blk,
                           name=f"nattn_l{layer}")
            xl = _outproj(o_lat, w_out, xl, mod_l, g0, b0, alpha=alpha, name=f"outproj_c_l{layer}_lat")
            if need_ctx:
                o_ctx = _cattn(p_ctx, heads=c_heads, name=f"cattn_l{layer}")
                xc = _outproj(o_ctx, w_out, xc, mod_c, g0, b0, alpha=alpha, name=f"outproj_c_l{layer}_ctx")

        wa = _pad_cols(ffn_w_up[layer, :, :d_ff], ffp).astype(BF16)
        wg = _pad_cols(ffn_w_up[layer, :, d_ff:], ffp).astype(BF16)
        wdw = _pad_cols(ffn_w_dw[layer], ffp)
        bdw = _pad_cols(ffn_b_dw[layer].reshape(1, d_ff), ffp)
        wdn = jnp.pad(ffn_w_down[layer], ((0, ffp - d_ff), (0, 0))).astype(BF16)
        g1, b1 = ln_g[layer, 1], ln_b[layer, 1]
        for is_ctx, xs, mods in ((True, xc, mod_c), (False, xl, mod_l)):
            if is_ctx and not need_ctx:
                continue
            tag = f"l{layer}_{'ctx' if is_ctx else 'lat'}"
            hmid = _ffn_up(xs, mods, wa, wg, wdw, bdw, name=f"ffn_up_{tag}")
            y = _ffn_down(hmid, wdn, xs, mods, g1, b1, alpha=alpha, name=f"ffn_down_{tag}")
            if is_ctx:
                xc = y
            else:
                xl = y
    return xl
```

```python
import functools

import numpy as np
import jax
import jax.numpy as jnp
from jax import lax
from jax.experimental import pallas as pl
from jax.experimental.pallas import tpu as pltpu

F32 = jnp.float32
BF16 = jnp.bfloat16

HEAD = 128
GRID_W = 64
WIN_R = 8
WIN_C = 16
ROPE_BASE = 10000.0
A_CHUNK = 32
B_CHUNK = 64
EPS = 1e-6
NEG = -1e30
HALO = 16
FF_ALIGN = 512
VMEM_LIMIT = 56 * 1024 * 1024


def _cparams(*sem):
    return pltpu.CompilerParams(dimension_semantics=sem, vmem_limit_bytes=VMEM_LIMIT)


def _pick(n, prefs):
    for p in prefs:
        if n % p == 0:
            return p
    return n


def _sigmoid(x):
    return 1.0 / (1.0 + jnp.exp(-x))


def _silu(x):
    return x * _sigmoid(x)


def _softplus(x):
    return jnp.maximum(x, 0.0) + jnp.log(1.0 + jnp.exp(-jnp.abs(x)))


def _dot(a, b):
    return jnp.dot(a.astype(BF16), b.astype(BF16), preferred_element_type=F32)


def _dot_nt(a, b):
    return lax.dot_general(a.astype(BF16), b.astype(BF16), (((1,), (1,)), ((), ())),
                           preferred_element_type=F32)


def _dot_tn(a, b):
    return lax.dot_general(a.astype(BF16), b.astype(BF16), (((0,), (0,)), ((), ())),
                           preferred_element_type=F32)


def _split2(x):
    hi = x.astype(BF16)
    return hi, (x - hi.astype(F32)).astype(BF16)


def _mm(a, b):
    return jnp.dot(a, b, preferred_element_type=F32)


def _dot3(a, b):
    ah, al = _split2(a)
    bh, bl = _split2(b)
    return _mm(ah, bh) + _mm(ah, bl) + _mm(al, bh)


def _split3(x):
    x1 = x.astype(BF16)
    r = x - x1.astype(F32)
    x2 = r.astype(BF16)
    return x1, x2, (r - x2.astype(F32)).astype(BF16)


def _tri_dot(tri, x):
    return sum(_mm(tri, xi) for xi in _split3(x))


def _dot_tri(x, tri):
    return sum(_mm(xi, tri) for xi in _split3(x))


def _layernorm_rows(r, g, b):
    mu = jnp.mean(r, axis=-1, keepdims=True)
    rc = r - mu
    var = jnp.mean(rc * rc, axis=-1, keepdims=True)
    return rc * lax.rsqrt(var + EPS) * g + b


def _ada_kernel(c_ref, w_ref, b_ref, o_ref):
    s = _silu(c_ref[...])
    o_ref[0] = _dot(s, w_ref[0]) + b_ref[0]


def _ada(cc, w_ada, b_ada):
    depth, d, n = w_ada.shape
    rows = cc.shape[0]
    tn = _pick(n, (1024, 512, 256, 128))
    return pl.pallas_call(
        _ada_kernel,
        grid=(depth, n // tn),
        in_specs=[pl.BlockSpec((rows, d), lambda l, j: (0, 0)),
                  pl.BlockSpec((1, d, tn), lambda l, j: (l, 0, j)),
                  pl.BlockSpec((1, 1, tn), lambda l, j: (l, 0, j))],
        out_specs=pl.BlockSpec((1, rows, tn), lambda l, j: (l, 0, j)),
        out_shape=jax.ShapeDtypeStruct((depth, rows, n), F32),
        compiler_params=_cparams("parallel", "parallel"),
        name="ada_modulation",
    )(cc, w_ada, b_ada.reshape(depth, 1, n))


def _proj_kernel(x_ref, mod_ref, w_ref, *rest, shift, scale, has_gate):
    if has_gate:
        wg_ref, o_ref, og_ref, xs_ref = rest
    else:
        o_ref, xs_ref = rest

    @pl.when(pl.program_id(2) == 0)
    def _():
        h = x_ref[0] * (1.0 + mod_ref[0, scale:scale + 1, :]) + mod_ref[0, shift:shift + 1, :]
        xs_ref[...] = h.astype(BF16)
        if has_gate:
            og_ref[0] = jnp.dot(xs_ref[...], wg_ref[...], preferred_element_type=F32)

    o_ref[0] = jnp.dot(xs_ref[...], w_ref[...], preferred_element_type=F32).astype(o_ref.dtype)


def _proj(x, mod, w, wg=None, *, shift, scale, out_dtype=F32, name):
    g, r, k = x.shape
    n = w.shape[1]
    tm = _pick(r, (1024, 512, 256, 128, 64))
    tn = _pick(n, (1024, 512, 256, 128))
    has_gate = wg is not None
    gm = mod.shape[0]
    in_specs = [pl.BlockSpec((1, tm, k), lambda a, i, j: (a, i, 0)),
                pl.BlockSpec((1, 6, k), lambda a, i, j: (a % gm, 0, 0)),
                pl.BlockSpec((k, tn), lambda a, i, j: (0, j))]
    out_specs = [pl.BlockSpec((1, tm, tn), lambda a, i, j: (a, i, j))]
    out_shape = [jax.ShapeDtypeStruct((g, r, n), out_dtype)]
    args = [x, mod, w]
    if has_gate:
        in_specs.append(pl.BlockSpec((k, HEAD), lambda a, i, j: (0, 0)))
        out_specs.append(pl.BlockSpec((1, tm, HEAD), lambda a, i, j: (a, i, 0)))
        out_shape.append(jax.ShapeDtypeStruct((g, r, HEAD), F32))
        args.append(wg)
    outs = pl.pallas_call(
        functools.partial(_proj_kernel, shift=shift, scale=scale, has_gate=has_gate),
        grid=(g, r // tm, n // tn),
        in_specs=in_specs, out_specs=out_specs, out_shape=out_shape,
        scratch_shapes=[pltpu.VMEM((tm, k), BF16)],
        compiler_params=_cparams("parallel", "parallel", "arbitrary"),
        name=name,
    )(*args)
    return outs if has_gate else outs[0]


def _merge_half(of_ref, ob_ref, gate_ref, norm_ref, ys_ref, col0, nheads):
    for h in range(nheads):
        hs = slice(h * HEAD, (h + 1) * HEAD)
        o = of_ref[0, :, hs] + ob_ref[0, :, hs]
        ms = jnp.mean(o * o, axis=-1, keepdims=True)
        y = o * lax.rsqrt(ms + EPS) * norm_ref[...] * _silu(gate_ref[0, :, hs])
        ys_ref[:, col0 + h * HEAD:col0 + (h + 1) * HEAD] = y.astype(BF16)


def _outproj_kernel(*refs, merge, alpha, a_heads, b_heads):
    if merge:
        (oaf, oab, obf, obb, ga, zb, an, bn, w_ref, res_ref, mod_ref, g_ref, b_ref, o_ref, ys_ref) = refs
        _merge_half(oaf, oab, ga, an, ys_ref, 0, a_heads)
        _merge_half(obf, obb, zb, bn, ys_ref, a_heads * HEAD, b_heads)
        y = ys_ref[...]
    else:
        x_ref, w_ref, res_ref, mod_ref, g_ref, b_ref, o_ref = refs
        y = x_ref[0]
    acc = jnp.dot(y, w_ref[...], preferred_element_type=F32)
    r = alpha * res_ref[0] + mod_ref[0, 2:3, :] * acc
    o_ref[0] = _layernorm_rows(r, g_ref[...], b_ref[...])


def _outproj(xs, w, res, mod, ln_g, ln_b, *, alpha, merge=None, name):
    g, r, d = res.shape
    k = w.shape[0]
    tm = _pick(r, (256, 128, 64))
    gm = mod.shape[0]
    row = lambda a, i: (a, i, 0)
    fixed = lambda a, i: (0, 0)
    if merge is None:
        in_specs = [pl.BlockSpec((1, tm, k), row)]
        args = [xs]
        scratch = []
        kern = functools.partial(_outproj_kernel, merge=False, alpha=alpha, a_heads=0, b_heads=0)
    else:
        oaf, oab, obf, obb, p, an, bn = xs
        a_val, b_val, ga_blk, zb_blk = merge
        in_specs = [pl.BlockSpec((1, tm, a_val), row), pl.BlockSpec((1, tm, a_val), row),
                    pl.BlockSpec((1, tm, b_val), row), pl.BlockSpec((1, tm, b_val), row),
                    pl.BlockSpec((1, tm, a_val), lambda a, i: (a, i, ga_blk)),
                    pl.BlockSpec((1, tm, b_val), lambda a, i: (a, i, zb_blk)),
                    pl.BlockSpec((1, HEAD), fixed), pl.BlockSpec((1, HEAD), fixed)]
        args = [oaf, oab, obf, obb, p, p, an, bn]
        scratch = [pltpu.VMEM((tm, k), BF16)]
        kern = functools.partial(_outproj_kernel, merge=True, alpha=alpha,
                                 a_heads=a_val // HEAD, b_heads=b_val // HEAD)
    in_specs += [pl.BlockSpec((k, d), fixed),
                 pl.BlockSpec((1, tm, d), row),
                 pl.BlockSpec((1, 6, d), lambda a, i: (a % gm, 0, 0)),
                 pl.BlockSpec((1, d), fixed), pl.BlockSpec((1, d), fixed)]
    args += [w, res, mod, ln_g.reshape(1, d), ln_b.reshape(1, d)]
    return pl.pallas_call(
        kern,
        grid=(g, r // tm),
        in_specs=in_specs,
        out_specs=pl.BlockSpec((1, tm, d), row),
        out_shape=jax.ShapeDtypeStruct((g, r, d), F32),
        scratch_shapes=scratch,
        compiler_params=_cparams("parallel", "parallel"),
        name=name,
    )(*args)


def _tri_masks(n):
    row = lax.broadcasted_iota(jnp.int32, (n, n), 0)
    col = lax.broadcasted_iota(jnp.int32, (n, n), 1)
    return col <= row, col >= row


def _gla_kernel(qf_ref, zf_ref, vf_ref, qb_ref, zb_ref, vb_ref, lb_ref, s0_ref,
                of_ref, ob_ref, sfin_ref, st_ref, *, hb, nchunk):
    t = pl.program_id(2)
    c = A_CHUNK

    @pl.when(t == 0)
    def _():
        st_ref[...] = s0_ref[0]

    causal, anti = _tri_masks(c)
    streams = ((qf_ref, zf_ref, vf_ref, of_ref, causal, c // 2 - 1, c - 1),
               (qb_ref, zb_ref, vb_ref, ob_ref, anti, c - c // 2, 0))

    def body(ci, carry):
        items = []
        for d, (q_ref, z_ref, v_ref, o_ref, mask, mid, last) in enumerate(streams):
            cc = ci if d == 0 else nchunk - 1 - ci
            sl = pl.ds(pl.multiple_of(cc * c, c), c)
            q = _silu(q_ref[0, sl, :])
            v = v_ref[0, sl, :]
            lb = lb_ref[d:d + 1, :]
            f = lb + (1.0 - lb) * _sigmoid(z_ref[0, sl, :])
            k = 1.0 - f
            cum = _tri_dot(mask.astype(BF16), jnp.log(f))
            m = cum[mid:mid + 1, :]
            bl = cum[last:last + 1, :]
            qe = q * jnp.exp(cum)
            qm = q * jnp.exp(cum - m)
            km = k * jnp.exp(m - cum)
            kd = k * jnp.exp(bl - cum)
            dec = jnp.exp(bl)
            for h in range(hb):
                hs = slice(h * HEAD, (h + 1) * HEAD)
                items.append(dict(d=d, h=h, sl=sl, hs=hs, o_ref=o_ref, mask=mask, qe=qe[:, hs], qm=qm[:, hs],
                                  km=km[:, hs], kd=kd[:, hs], v=v[:, hs], dec=dec[:, hs]))
        for it in items:
            it["a"] = jnp.where(it["mask"], _dot_nt(it["qm"], it["km"]), 0.0)
            it["u"] = _dot_tn(it["v"], it["kd"])
        for it in items:
            it["st"] = st_ref[it["d"], it["h"]]
            it["o"] = _dot_nt(it["qe"], it["st"]) + _dot(it["a"], it["v"])
        for it in items:
            it["o_ref"][0, it["sl"], it["hs"]] = it["o"]
            st_ref[it["d"], it["h"]] = it["st"] * it["dec"] + it["u"]
        return carry

    lax.fori_loop(0, nchunk, body, 0)

    @pl.when(t == pl.num_programs(2) - 1)
    def _():
        sfin_ref[0] = st_ref[...]


def _gla(p, lb, s0, *, a_key, name):
    b, l, _ = p.shape
    heads = a_key // HEAD
    hb = _pick(heads, (8, 4, 2, 1))
    wb = hb * HEAD
    per = a_key // wb
    tb = _pick(l, (256, 128, 64, 32))
    nblk = l // tb
    fwd = lambda grp: (lambda bi, hg, t: (bi, t, grp * per + hg))
    bwd = lambda grp: (lambda bi, hg, t: (bi, nblk - 1 - t, grp * per + hg))
    blk = (1, tb, wb)
    st_blk = (1, 2, hb, HEAD, HEAD)
    st_map = lambda bi, hg, t: (bi, 0, hg, 0, 0)
    return pl.pallas_call(
        functools.partial(_gla_kernel, hb=hb, nchunk=tb // A_CHUNK),
        grid=(b, heads // hb, nblk),
        in_specs=[pl.BlockSpec(blk, fwd(0)), pl.BlockSpec(blk, fwd(1)), pl.BlockSpec(blk, fwd(3)),
                  pl.BlockSpec(blk, bwd(0)), pl.BlockSpec(blk, bwd(2)), pl.BlockSpec(blk, bwd(3)),
                  pl.BlockSpec((2, wb), lambda bi, hg, t: (0, hg)),
                  pl.BlockSpec(st_blk, st_map)],
        out_specs=[pl.BlockSpec(blk, lambda bi, hg, t: (bi, t, hg)),
                   pl.BlockSpec(blk, lambda bi, hg, t: (bi, nblk - 1 - t, hg)),
                   pl.BlockSpec(st_blk, st_map)],
        out_shape=[jax.ShapeDtypeStruct((b, l, a_key), F32),
                   jax.ShapeDtypeStruct((b, l, a_key), F32),
                   jax.ShapeDtypeStruct(s0.shape, F32)],
        scratch_shapes=[pltpu.VMEM((2, hb, HEAD, HEAD), F32)],
        compiler_params=_cparams("parallel", "parallel", "arbitrary"),
        name=name,
    )(p, p, p, p, p, p, lb, s0)


def _gdn_prep_kernel(x_ref, w_ref, o_ref, *, qk_blocks, q_blocks, width):
    x = x_ref[0]
    l = x.shape[0]
    pos = lax.broadcasted_iota(jnp.int32, x.shape, 0)
    pad = width // 2
    acc = x * w_ref[pad:pad + 1, :]
    for j in range(width):
        s = j - pad
        if s == 0:
            continue
        shifted = pltpu.roll(x, (-s) % l, 0)
        ok = (pos + s >= 0) & (pos + s < l)
        acc = acc + jnp.where(ok, shifted, 0.0) * w_ref[j:j + 1, :]
    y = _silu(acc)
    cb = pl.program_id(1)
    inv = lax.rsqrt(jnp.sum(y * y, axis=-1, keepdims=True) + EPS)
    inv = jnp.where(cb < qk_blocks, inv, 1.0) * jnp.where(cb < q_blocks, HEAD ** -0.5, 1.0)
    o_ref[0] = y * inv


def _gdn_prep(p, conv_w, *, col0, b_key, name):
    b, l, _ = p.shape
    width, c = conv_w.shape
    nb = c // HEAD
    off = col0 // HEAD
    return pl.pallas_call(
        functools.partial(_gdn_prep_kernel, qk_blocks=2 * b_key // HEAD, q_blocks=b_key // HEAD, width=width),
        grid=(b, nb),
        in_specs=[pl.BlockSpec((1, l, HEAD), lambda bi, cb: (bi, 0, off + cb)),
                  pl.BlockSpec((width, HEAD), lambda bi, cb: (0, cb))],
        out_specs=pl.BlockSpec((1, l, HEAD), lambda bi, cb: (bi, 0, cb)),
        out_shape=jax.ShapeDtypeStruct((b, l, c), F32),
        compiler_params=_cparams("parallel", "parallel"),
        name=name,
    )(p, conv_w)


def _gdn_kernel(qf_ref, kf_ref, vf_ref, gcf_ref, grf_ref, qb_ref, kb_ref, vb_ref, gcb_ref, grb_ref,
                pc_ref, pr_ref, s0_ref, of_ref, ob_ref, sfin_ref, st_ref, *, hb, nchunk):
    t = pl.program_id(2)
    c = B_CHUNK

    @pl.when(t == 0)
    def _():
        st_ref[...] = s0_ref[0]

    row = lax.broadcasted_iota(jnp.int32, (c, HEAD), 0)
    col = lax.broadcasted_iota(jnp.int32, (c, HEAD), 1)
    right = col >= c
    eye_right = (col == row + c).astype(F32)
    causal, anti = _tri_masks(c)
    zeros_k = jnp.zeros((c, HEAD), F32)
    dirs = ((col <= row, col < row, causal.astype(BF16), ((col >= row) & ~right).astype(BF16), c - 1),
            ((col >= row) & ~right, (col > row) & ~right, anti.astype(BF16), (col <= row).astype(BF16), 0))
    streams = ((qf_ref, kf_ref, vf_ref, gcf_ref, grf_ref, of_ref), (qb_ref, kb_ref, vb_ref, gcb_ref, grb_ref, ob_ref))

    def body(ci, carry):
        items = []
        for d in range(2):
            q_ref, k_ref, v_ref, gc_ref, gr_ref, o_ref = streams[d]
            incl, strict, tri_c, tri_r, last = dirs[d]
            cc = ci if d == 0 else nchunk - 1 - ci
            sl = pl.ds(pl.multiple_of(cc * c, c), c)
            gcol = gc_ref[0, 0, sl, :]
            a_log = pr_ref[0, d:d + 1, 0:hb]
            dt = pr_ref[0, 2 + d:3 + d, 0:hb]
            g_col = _tri_dot(tri_c, -jnp.exp(a_log) * _softplus(gcol[:, d * hb:(d + 1) * hb] + dt))
            beta_col = _sigmoid(gcol[:, (2 + d) * hb:(3 + d) * hb])
            grow = gr_ref[0, 0, cc]
            a_log_c = pc_ref[0, :, d:d + 1]
            dt_c = pc_ref[0, :, 2 + d:3 + d]
            g_row = _dot_tri(-jnp.exp(a_log_c) * _softplus(grow[d * hb:(d + 1) * hb, :] + dt_c), tri_r)
            for hl in range(hb):
                hs = slice(hl * HEAD, (hl + 1) * HEAD)
                items.append(dict(d=d, hl=hl, sl=sl, hs=hs, o_ref=o_ref, incl=incl, strict=strict, last=last,
                                  gc=g_col[:, hl:hl + 1], bc=beta_col[:, hl:hl + 1], gr=g_row[hl:hl + 1, :],
                                  q=q_ref[0, sl, hs], k=k_ref[0, sl, hs], v=v_ref[0, sl, hs]))
        for it in items:
            it["decay"] = jnp.exp(jnp.where(it["incl"], it["gc"] - it["gr"], NEG))
            it["kb"] = it["k"] * it["bc"]
        for it in items:
            x = jnp.concatenate([it["kb"], it["q"]], axis=0)
            kpad = jnp.concatenate([it["k"], zeros_k], axis=0)
            it["kq"] = _dot_nt(x, kpad)
        for it in items:
            it["w"] = eye_right - jnp.where(it["strict"], it["kq"][:c] * it["decay"], 0.0)
            it["attn"] = it["kq"][c:] * it["decay"]
        for _ in range(int(np.log2(c))):
            rs = [_dot3(it["w"][:, :c], it["w"]) for it in items]
            for it, r in zip(items, rs):
                it["w"] = r + jnp.where(right, it["w"], 0.0)
        for it in items:
            tinv = pltpu.roll(it["w"], c, 1)[:, :c]
            egc = jnp.exp(it["gc"])
            rhs = jnp.concatenate([it["v"] * it["bc"], it["kb"] * egc], axis=1)
            it["sol"] = _dot3(tinv, rhs)
            g_last = it["gc"][it["last"]:it["last"] + 1, :]
            it["qe"] = it["q"] * egc
            it["kdt"] = jnp.transpose(it["k"] * jnp.exp(g_last - it["gc"]))
            it["dlast"] = jnp.exp(g_last)
        for it in items:
            it["s"] = st_ref[it["d"], it["hl"]]
            it["r"] = _dot(jnp.concatenate([it["sol"][:, HEAD:], it["qe"]], axis=0), it["s"])
        for it in items:
            it["vn"] = it["sol"][:, :HEAD] - it["r"][:c]
            it["r2"] = _dot(jnp.concatenate([it["attn"][:, :c], it["kdt"]], axis=0), it["vn"])
        for it in items:
            it["o_ref"][0, it["sl"], it["hs"]] = it["r"][c:] + it["r2"][:c]
            st_ref[it["d"], it["hl"]] = it["s"] * it["dlast"] + it["r2"][c:]
        return carry

    lax.fori_loop(0, nchunk, body, 0)

    @pl.when(t == pl.num_programs(2) - 1)
    def _():
        sfin_ref[0] = st_ref[...]


def _gdn(qkv, gates, prm, s0, *, b_key, name):
    b, l, _ = qkv.shape
    assert HEAD == 2 * B_CHUNK
    heads = b_key // HEAD
    hb = _pick(heads, (4, 2, 1))
    nhg = heads // hb
    wb = hb * HEAD
    tb = _pick(l, (256, 128, 64))
    nblk = l // tb
    cpb = tb // B_CHUNK
    g4 = gates[:, :, :4 * heads].reshape(b, l, 4, nhg, hb)
    gcol = jnp.transpose(g4, (0, 3, 1, 2, 4)).reshape(b, nhg, l, 4 * hb)
    grow = jnp.swapaxes(gcol.reshape(b, nhg, l // B_CHUNK, B_CHUNK, 4 * hb), 3, 4)
    gcol = jnp.pad(gcol, ((0, 0), (0, 0), (0, 0), (0, HEAD - 4 * hb)))
    p4 = prm.reshape(4, nhg, hb)
    prow = jnp.pad(jnp.transpose(p4, (1, 0, 2)), ((0, 0), (0, 0), (0, HEAD - hb)))
    pcol = jnp.pad(jnp.transpose(p4, (1, 2, 0)), ((0, 0), (0, 0), (0, HEAD - 4)))
    fwd = lambda grp: (lambda bi, hg, t: (bi, t, grp * nhg + hg))
    bwd = lambda grp: (lambda bi, hg, t: (bi, nblk - 1 - t, grp * nhg + hg))
    blk = (1, tb, wb)
    gc_blk = (1, 1, tb, HEAD)
    gr_blk = (1, 1, cpb, 4 * hb, B_CHUNK)
    st_blk = (1, 2, hb, HEAD, HEAD)
    st_map = lambda bi, hg, t: (bi, 0, hg, 0, 0)
    return pl.pallas_call(
        functools.partial(_gdn_kernel, hb=hb, nchunk=cpb),
        grid=(b, nhg, nblk),
        in_specs=[pl.BlockSpec(blk, fwd(0)), pl.BlockSpec(blk, fwd(1)), pl.BlockSpec(blk, fwd(2)),
                  pl.BlockSpec(gc_blk, lambda bi, hg, t: (bi, hg, t, 0)),
                  pl.BlockSpec(gr_blk, lambda bi, hg, t: (bi, hg, t, 0, 0)),
                  pl.BlockSpec(blk, bwd(0)), pl.BlockSpec(blk, bwd(1)), pl.BlockSpec(blk, bwd(2)),
                  pl.BlockSpec(gc_blk, lambda bi, hg, t: (bi, hg, nblk - 1 - t, 0)),
                  pl.BlockSpec(gr_blk, lambda bi, hg, t: (bi, hg, nblk - 1 - t, 0, 0)),
                  pl.BlockSpec((1, hb, HEAD), lambda bi, hg, t: (hg, 0, 0)),
                  pl.BlockSpec((1, 4, HEAD), lambda bi, hg, t: (hg, 0, 0)),
                  pl.BlockSpec(st_blk, st_map)],
        out_specs=[pl.BlockSpec(blk, lambda bi, hg, t: (bi, t, hg)),
                   pl.BlockSpec(blk, lambda bi, hg, t: (bi, nblk - 1 - t, hg)),
                   pl.BlockSpec(st_blk, st_map)],
        out_shape=[jax.ShapeDtypeStruct((b, l, b_key), F32),
                   jax.ShapeDtypeStruct((b, l, b_key), F32),
                   jax.ShapeDtypeStruct(s0.shape, F32)],
        scratch_shapes=[pltpu.VMEM((2, hb, HEAD, HEAD), F32)],
        compiler_params=_cparams("parallel", "parallel", "arbitrary"),
        name=name,
    )(qkv, qkv, qkv, gcol, grow, qkv, qkv, qkv, gcol, grow, pcol, prow, s0)


def _rope(x, cos, s1, s2):
    q = HEAD // 4
    return x * cos + pltpu.roll(x, HEAD - q, 1) * s1 + pltpu.roll(x, q, 1) * s2


def _rope_tables(l):
    quarter = HEAD // 4
    pos = jnp.arange(l)
    inv = ROPE_BASE ** (-jnp.arange(quarter, dtype=F32) / quarter)
    zero = jnp.zeros((l, quarter), F32)
    cos, s1, s2 = [], [], []
    for p in ((pos // GRID_W).astype(F32), (pos % GRID_W).astype(F32)):
        ang = p[:, None] * inv[None, :]
        cos += [jnp.cos(ang), jnp.cos(ang)]
        s1 += [-jnp.sin(ang), zero]
        s2 += [zero, jnp.sin(ang)]
    return (jnp.concatenate(cos, -1), jnp.concatenate(s1, -1), jnp.concatenate(s2, -1))


def _nattn_plan(rows):
    wr = min(WIN_R, rows)
    tr = min(4, rows)
    kbr = min(rows, tr + wr)
    qc = np.arange(GRID_W)
    c0 = np.clip(qc - WIN_C // 2, 0, GRID_W - WIN_C)
    ok_c = (qc[None, :] >= c0[:, None]) & (qc[None, :] < c0[:, None] + WIN_C)
    dc = np.clip(qc[None, :] - qc[:, None] + WIN_C - 1, 0, 2 * WIN_C - 2)
    tiles, classes, keys = [], [], {}
    for t in range(rows // tr):
        ks = int(np.clip(tr * t - wr // 2, 0, rows - kbr))
        qr = tr * t + np.arange(tr)
        kr = ks + np.arange(kbr)
        r0 = np.clip(qr - wr // 2, 0, rows - wr)
        ok_r = (kr[None, :] >= r0[:, None]) & (kr[None, :] < r0[:, None] + wr)
        dr = np.clip(kr[None, :] - qr[:, None] + WIN_R - 1, 0, 2 * WIN_R - 2)
        valid = (ok_r[:, None, :, None] & ok_c[None, :, None, :]).reshape(tr * GRID_W, kbr * GRID_W)
        key = dr.tobytes() + valid.tobytes()
        if key not in keys:
            keys[key] = len(classes)
            classes.append((dr, valid))
        tiles.append((tr * t * GRID_W, ks * GRID_W, keys[key]))
    dr = np.stack([cl[0] for cl in classes]).astype(np.int32)
    valid = np.stack([cl[1] for cl in classes])
    return tuple(tiles), dr, dc.astype(np.int32), valid


def _nattn_bias(rel_bias, dr, dc, valid):
    h = rel_bias.shape[0]
    ncls, tr, kbr = dr.shape
    w = dc.shape[0]
    by_col = jnp.take(rel_bias, dc.reshape(-1), axis=2).reshape(h, -1, w, w)
    tab = jnp.take(by_col, dr.reshape(-1), axis=1).reshape(h, ncls, tr, kbr, w, w)
    tab = jnp.transpose(tab, (0, 1, 2, 4, 3, 5)).reshape(h, ncls, tr * w, kbr * w)
    return jnp.where(valid[None], tab, NEG)


def _nattn_kernel(q_ref, k_ref, v_ref, kc_ref, vc_ref, bias_ref, cos_ref, s1_ref, s2_ref,
                  o_ref, kr_ref, vr_ref, *, tiles):
    tq, tk = bias_ref.shape[2], bias_ref.shape[3]
    kr_ref[...] = _rope(k_ref[0], cos_ref[...], s1_ref[...], s2_ref[...]).astype(BF16)
    vr_ref[...] = v_ref[0].astype(BF16)
    kc = kc_ref[0].astype(BF16)
    vc = vc_ref[0].astype(BF16)
    for q0, k0, cls in tiles:
        qs = slice(q0, q0 + tq)
        qt = _rope(q_ref[0, qs, :], cos_ref[qs, :], s1_ref[qs, :], s2_ref[qs, :]) * HEAD ** -0.5
        qt = qt.astype(BF16)
        s_lat = _dot_nt(qt, kr_ref[k0:k0 + tk, :]) + bias_ref[0, cls]
        s_ctx = _dot_nt(qt, kc)
        m = jnp.maximum(jnp.max(s_lat, axis=-1, keepdims=True), jnp.max(s_ctx, axis=-1, keepdims=True))
        p_lat = jnp.exp(s_lat - m)
        p_ctx = jnp.exp(s_ctx - m)
        den = jnp.sum(p_lat, axis=-1, keepdims=True) + jnp.sum(p_ctx, axis=-1, keepdims=True)
        o = _dot(p_lat, vr_ref[k0:k0 + tk, :]) + _dot(p_ctx, vc)
        o_ref[0, qs, :] = (o / den).astype(o_ref.dtype)


def _nattn(p_lat, p_ctx, bias, tiles, ropes, *, heads, kc_blk, vc_blk, name):
    b, l, _ = p_lat.shape
    lc = p_ctx.shape[1]
    ncls, tq, tk = bias.shape[1:]
    tab = pl.BlockSpec((l, HEAD), lambda h, bi: (0, 0))
    return pl.pallas_call(
        functools.partial(_nattn_kernel, tiles=tiles),
        grid=(heads, b),
        in_specs=[pl.BlockSpec((1, l, HEAD), lambda h, bi: (bi, 0, h)),
                  pl.BlockSpec((1, l, HEAD), lambda h, bi: (bi, 0, heads + h)),
                  pl.BlockSpec((1, l, HEAD), lambda h, bi: (bi, 0, 2 * heads + h)),
                  pl.BlockSpec((1, lc, HEAD), lambda h, bi: (bi, 0, kc_blk + h)),
                  pl.BlockSpec((1, lc, HEAD), lambda h, bi: (bi, 0, vc_blk + h)),
                  pl.BlockSpec((1, ncls, tq, tk), lambda h, bi: (h, 0, 0, 0)),
                  tab, tab, tab],
        out_specs=pl.BlockSpec((1, l, HEAD), lambda h, bi: (bi, 0, h)),
        out_shape=jax.ShapeDtypeStruct((b, l, heads * HEAD), BF16),
        scratch_shapes=[pltpu.VMEM((l, HEAD), BF16), pltpu.VMEM((l, HEAD), BF16)],
        compiler_params=_cparams("parallel", "parallel"),
        name=name,
    )(p_lat, p_lat, p_lat, p_ctx, p_ctx, bias, *ropes)


def _cattn_kernel(q_ref, k_ref, v_ref, o_ref):
    s = _dot_nt(q_ref[0] * HEAD ** -0.5, k_ref[0])
    p = jnp.exp(s - jnp.max(s, axis=-1, keepdims=True))
    o = _dot(p, v_ref[0]) / jnp.sum(p, axis=-1, keepdims=True)
    o_ref[0] = o.astype(o_ref.dtype)


def _cattn(p_ctx, *, heads, name):
    b, lc, _ = p_ctx.shape
    blk = (1, lc, HEAD)
    return pl.pallas_call(
        _cattn_kernel,
        grid=(b, heads),
        in_specs=[pl.BlockSpec(blk, lambda bi, h: (bi, 0, h)),
                  pl.BlockSpec(blk, lambda bi, h: (bi, 0, heads + h)),
                  pl.BlockSpec(blk, lambda bi, h: (bi, 0, 2 * heads + h))],
        out_specs=pl.BlockSpec(blk, lambda bi, h: (bi, 0, h)),
        out_shape=jax.ShapeDtypeStruct((b, lc, heads * HEAD), BF16),
        compiler_params=_cparams("parallel", "parallel"),
        name=name,
    )(p_ctx, p_ctx, p_ctx)


def _ffn_up_kernel(x_ref, xp_ref, xn_ref, mod_ref, wa_ref, wg_ref, wdw_ref, bdw_ref, o_ref, xs_ref):
    tm = x_ref.shape[1]
    i = pl.program_id(1)

    @pl.when(pl.program_id(2) == 0)
    def _():
        sc = 1.0 + mod_ref[0, 4:5, :]
        sh = mod_ref[0, 3:4, :]
        xs_ref[0:HALO, :] = (xp_ref[0] * sc + sh).astype(BF16)
        xs_ref[HALO:HALO + tm, :] = (x_ref[0] * sc + sh).astype(BF16)
        xs_ref[HALO + tm:, :] = (xn_ref[0] * sc + sh).astype(BF16)

    a = jnp.dot(xs_ref[...], wa_ref[...], preferred_element_type=F32)
    gate = jnp.dot(xs_ref[HALO:HALO + tm, :], wg_ref[...], preferred_element_type=F32)
    ext = tm + 2 * HALO
    a_prev = pltpu.roll(a, 1, 0)[HALO:HALO + tm]
    a_next = pltpu.roll(a, ext - 1, 0)[HALO:HALO + tm]
    pos = lax.broadcasted_iota(jnp.int32, gate.shape, 0)
    a_prev = jnp.where((pos == 0) & (i == 0), 0.0, a_prev)
    a_next = jnp.where((pos == tm - 1) & (i == pl.num_programs(1) - 1), 0.0, a_next)
    conv = (a_prev * wdw_ref[0:1, :] + a[HALO:HALO + tm] * wdw_ref[1:2, :] + a_next * wdw_ref[2:3, :]
            + bdw_ref[...])
    gelu = 0.5 * conv * (1.0 + lax.erf(conv * np.float32(np.sqrt(0.5))))
    o_ref[0] = (gelu * gate).astype(o_ref.dtype)


def _ffn_up(x, mod, wa, wg, wdw, bdw, *, name):
    g, r, k = x.shape
    f = wa.shape[1]
    tm = _pick(r, (1024, 512, 256, 128, 64, 32, 16))
    tn = _pick(f, (512, 256, 128))
    hpt = tm // HALO
    nh = r // HALO
    gm = mod.shape[0]
    return pl.pallas_call(
        _ffn_up_kernel,
        grid=(g, r // tm, f // tn),
        in_specs=[pl.BlockSpec((1, tm, k), lambda a, i, j: (a, i, 0)),
                  pl.BlockSpec((1, HALO, k), lambda a, i, j: (a, jnp.maximum(i * hpt - 1, 0), 0)),
                  pl.BlockSpec((1, HALO, k), lambda a, i, j: (a, jnp.minimum((i + 1) * hpt, nh - 1), 0)),
                  pl.BlockSpec((1, 6, k), lambda a, i, j: (a % gm, 0, 0)),
                  pl.BlockSpec((k, tn), lambda a, i, j: (0, j)),
                  pl.BlockSpec((k, tn), lambda a, i, j: (0, j)),
                  pl.BlockSpec((3, tn), lambda a, i, j: (0, j)),
                  pl.BlockSpec((1, tn), lambda a, i, j: (0, j))],
        out_specs=pl.BlockSpec((1, tm, tn), lambda a, i, j: (a, i, j)),
        out_shape=jax.ShapeDtypeStruct((g, r, f), BF16),
        scratch_shapes=[pltpu.VMEM((tm + 2 * HALO, k), BF16)],
        compiler_params=_cparams("parallel", "parallel", "arbitrary"),
        name=name,
    )(x, x, x, mod, wa, wg, wdw, bdw)


def _ffn_down_kernel(x_ref, w_ref, res_ref, mod_ref, g_ref, b_ref, o_ref, acc_ref, *, alpha):
    kk = pl.program_id(2)

    @pl.when(kk == 0)
    def _():
        acc_ref[...] = jnp.zeros_like(acc_ref)

    acc_ref[...] += jnp.dot(x_ref[0], w_ref[...], preferred_element_type=F32)

    @pl.when(kk == pl.num_programs(2) - 1)
    def _():
        r = alpha * res_ref[0] + mod_ref[0, 5:6, :] * acc_ref[...]
        o_ref[0] = _layernorm_rows(r, g_ref[...], b_ref[...])


def _ffn_down(x, w, res, mod, ln_g, ln_b, *, alpha, name):
    g, r, f = x.shape
    d = w.shape[1]
    tm = _pick(r, (512, 256, 128, 64))
    tk = _pick(f, (512, 256, 128))
    gm = mod.shape[0]
    row = lambda a, i, kk: (a, i, 0)
    fixed = lambda a, i, kk: (0, 0)
    return pl.pallas_call(
        functools.partial(_ffn_down_kernel, alpha=alpha),
        grid=(g, r // tm, f // tk),
        in_specs=[pl.BlockSpec((1, tm, tk), lambda a, i, kk: (a, i, kk)),
                  pl.BlockSpec((tk, d), lambda a, i, kk: (kk, 0)),
                  pl.BlockSpec((1, tm, d), row),
                  pl.BlockSpec((1, 6, d), lambda a, i, kk: (a % gm, 0, 0)),
                  pl.BlockSpec((1, d), fixed), pl.BlockSpec((1, d), fixed)],
        out_specs=pl.BlockSpec((1, tm, d), row),
        out_shape=jax.ShapeDtypeStruct((g, r, d), F32),
        scratch_shapes=[pltpu.VMEM((tm, d), F32)],
        compiler_params=_cparams("parallel", "parallel", "arbitrary"),
        name=name,
    )(x, w, res, mod, ln_g.reshape(1, d), ln_b.reshape(1, d))


def _pad_cols(w, n):
    return jnp.pad(w, ((0, 0), (0, n - w.shape[1])))


def kernel(x, c, ctx, c_ctx, w_ada, b_ada, ln_g, ln_b, w_in_ab, hgrn_lb, hgrn_norm, gdn_conv, gdn_a_log,
           gdn_dt_bias, gdn_norm, w_out_ab, w_in_c, na_rel_bias, w_out_c, ffn_w_up, ffn_w_dw, ffn_b_dw,
           ffn_w_down):
    b, l, d = x.shape
    lc = ctx.shape[1]
    depth = w_ada.shape[0]
    alpha = (2 * depth) ** 0.25
    a_key = hgrn_lb.shape[-1]
    b_heads = gdn_a_log.shape[-1]
    b_key = b_heads * HEAD
    b_val = gdn_conv.shape[-1] - 2 * b_key
    a_val = w_out_ab.shape[1] - b_val
    n_main = 3 * a_key + 2 * a_val + 2 * b_key + 2 * b_val
    c_heads = d // HEAD
    d_ff = ffn_w_down.shape[1]
    ffp = -(-d_ff // FF_ALIGN) * FF_ALIGN
    assert a_key == a_val == b_key == b_val and w_in_ab.shape[-1] == n_main + 4 * b_heads
    assert l % (4 * GRID_W) == 0 or l // GRID_W < 4

    rows = -(-(b + 1) // 8) * 8
    cc = jnp.concatenate([c, c_ctx[None, :], jnp.zeros((rows - b - 1, d), F32)], axis=0)
    mod = _ada(cc, w_ada, b_ada).reshape(depth, rows, 6, d)

    lb_all = jax.nn.softmax(hgrn_lb.astype(F32), axis=0)
    lb_all = jnp.cumsum(lb_all, axis=0) - lb_all[0]

    tiles, bias_dr, bias_dc, bias_ok = _nattn_plan(l // GRID_W)
    ropes = _rope_tables(l)
    s_zero = jnp.zeros((b, 2, a_key // HEAD, HEAD, HEAD), F32)

    xl, xc = x, ctx
    for layer in range(depth):
        need_ctx = layer < depth - 1
        mod_l = mod[layer, :b]
        mod_c = mod[layer, b:b + 1]
        g0, b0 = ln_g[layer, 0], ln_b[layer, 0]
        if layer % 2 == 0:
            e = layer // 2
            w_main = w_in_ab[e, :, :n_main].astype(BF16)
            w_gate = _pad_cols(w_in_ab[e, :, n_main:], HEAD).astype(BF16)
            w_out = w_out_ab[e].astype(BF16)
            an = hgrn_norm[e].reshape(1, HEAD)
            bn = gdn_norm[e].reshape(1, HEAD)
            gdn_prm = jnp.concatenate([gdn_a_log[e], gdn_dt_bias[e]], axis=0)
            merge = (a_val, b_val, (3 * a_key + a_val) // a_val, (n_main - b_val) // b_val)
            sa, sb = s_zero, s_zero
            for is_ctx, xs, mods in ((True, xc, mod_c), (False, xl, mod_l)):
                tag = f"l{layer}_{'ctx' if is_ctx else 'lat'}"
                p, gates = _proj(xs, mods, w_main, w_gate, shift=0, scale=1,
                                 name=f"inproj_ab_{tag}")
                oaf, oab, sa = _gla(p, lb_all[e], sa, a_key=a_key, name=f"hgrn2_{tag}")
                qkv = _gdn_prep(p, gdn_conv[e], col0=3 * a_key + 2 * a_val, b_key=b_key, name=f"gdn_prep_{tag}")
                obf, obb, sb = _gdn(qkv, gates, gdn_prm, sb, b_key=b_key, name=f"gdn_{tag}")
                if is_ctx and not need_ctx:
                    continue
                y = _outproj((oaf, oab, obf, obb, p, an, bn), w_out, xs, mods,
                             g0, b0, alpha=alpha, merge=merge, name=f"outproj_ab_{tag}")
                if is_ctx:
                    xc = y
                else:
                    xl = y
        else:
            o = layer // 2
            w_in = w_in_c[o].astype(BF16)
            w_out = w_out_c[o].astype(BF16)
            bias = _nattn_bias(na_rel_bias[o], bias_dr, bias_dc, bias_ok)
            p_lat = _proj(xl, mod_l, w_in, shift=0, scale=1, name=f"inproj_c_l{layer}_lat")
            if need_ctx:
                p_ctx = _proj(xc, mod_c, w_in, shift=0, scale=1, name=f"inproj_c_l{layer}_ctx")
                kc_blk, vc_blk = c_heads, 2 * c_heads
            else:
                p_ctx = _proj(xc, mod_c, w_in[:, d:], shift=0, scale=1, name=f"inproj_c_l{layer}_ctx")
                kc_blk, vc_blk = 0, c_heads
            o_lat = _nattn(p_lat, p_ctx, bias, tiles, ropes, heads=c_heads, kc_blk=kc_blk, vc_blk=vc_blk,
                           name=f"nattn_l{layer}")
            xl = _outproj(o_lat, w_out, xl, mod_l, g0, b0, alpha=alpha, name=f"outproj_c_l{layer}_lat")
            if need_ctx:
                o_ctx = _cattn(p_ctx, heads=c_heads, name=f"cattn_l{layer}")
                xc = _outproj(o_ctx, w_out, xc, mod_c, g0, b0, alpha=alpha, name=f"outproj_c_l{layer}_ctx")

        wa = _pad_cols(ffn_w_up[layer, :, :d_ff], ffp).astype(BF16)
        wg = _pad_cols(ffn_w_up[layer, :, d_ff:], ffp).astype(BF16)
        wdw = _pad_cols(ffn_w_dw[layer], ffp)
        bdw = _pad_cols(ffn_b_dw[layer].reshape(1, d_ff), ffp)
        wdn = jnp.pad(ffn_w_down[layer], ((0, ffp - d_ff), (0, 0))).astype(BF16)
        g1, b1 = ln_g[layer, 1], ln_b[layer, 1]
        for is_ctx, xs, mods in ((True, xc, mod_c), (False, xl, mod_l)):
            if is_ctx and not need_ctx:
                continue
            tag = f"l{layer}_{'ctx' if is_ctx else 'lat'}"
            hmid = _ffn_up(xs, mods, wa, wg, wdw, bdw, name=f"ffn_up_{tag}")
            y = _ffn_down(hmid, wdn, xs, mods, g1, b1, alpha=alpha, name=f"ffn_down_{tag}")
            if is_ctx:
                xc = y
            else:
                xl = y
    return xl
```

```python
import functools

import numpy as np
import jax
import jax.numpy as jnp
from jax import lax
from jax.experimental import pallas as pl
from jax.experimental.pallas import tpu as pltpu

F32 = jnp.float32
BF16 = jnp.bfloat16

HEAD = 128
GRID_W = 64
WIN_R = 8
WIN_C = 16
ROPE_BASE = 10000.0
A_CHUNK = 32
B_CHUNK = 64
EPS = 1e-6
NEG = -1e30
HALO = 16
FF_ALIGN = 512
VMEM_LIMIT = 56 * 1024 * 1024


def _cparams(*sem):
    return pltpu.CompilerParams(dimension_semantics=sem, vmem_limit_bytes=VMEM_LIMIT)


def _pick(n, prefs):
    for p in prefs:
        if n % p == 0:
            return p
    return n


def _sigmoid(x):
    return 1.0 / (1.0 + jnp.exp(-x))


def _silu(x):
    return x * _sigmoid(x)


def _softplus(x):
    return jnp.maximum(x, 0.0) + jnp.log(1.0 + jnp.exp(-jnp.abs(x)))


def _dot(a, b):
    return jnp.dot(a.astype(BF16), b.astype(BF16), preferred_element_type=F32)


def _dot_nt(a, b):
    return lax.dot_general(a.astype(BF16), b.astype(BF16), (((1,), (1,)), ((), ())),
                           preferred_element_type=F32)


def _dot_tn(a, b):
    return lax.dot_general(a.astype(BF16), b.astype(BF16), (((0,), (0,)), ((), ())),
                           preferred_element_type=F32)


def _split2(x):
    hi = x.astype(BF16)
    return hi, (x - hi.astype(F32)).astype(BF16)


def _mm(a, b):
    return jnp.dot(a, b, preferred_element_type=F32)


def _dot3(a, b):
    ah, al = _split2(a)
    bh, bl = _split2(b)
    return _mm(ah, bh) + _mm(ah, bl) + _mm(al, bh)


def _split3(x):
    x1 = x.astype(BF16)
    r = x - x1.astype(F32)
    x2 = r.astype(BF16)
    return x1, x2, (r - x2.astype(F32)).astype(BF16)


def _tri_dot(tri, x):
    return sum(_mm(tri, xi) for xi in _split3(x))


def _dot_tri(x, tri):
    return sum(_mm(xi, tri) for xi in _split3(x))


def _layernorm_rows(r, g, b):
    mu = jnp.mean(r, axis=-1, keepdims=True)
    rc = r - mu
    var = jnp.mean(rc * rc, axis=-1, keepdims=True)
    return rc * lax.rsqrt(var + EPS) * g + b


def _ada_kernel(c_ref, w_ref, b_ref, o_ref):
    s = _silu(c_ref[...])
    o_ref[0] = _dot(s, w_ref[0]) + b_ref[0]


def _ada(cc, w_ada, b_ada):
    depth, d, n = w_ada.shape
    rows = cc.shape[0]
    tn = _pick(n, (1024, 512, 256, 128))
    return pl.pallas_call(
        _ada_kernel,
        grid=(depth, n // tn),
        in_specs=[pl.BlockSpec((rows, d), lambda l, j: (0, 0)),
                  pl.BlockSpec((1, d, tn), lambda l, j: (l, 0, j)),
                  pl.BlockSpec((1, 1, tn), lambda l, j: (l, 0, j))],
        out_specs=pl.BlockSpec((1, rows, tn), lambda l, j: (l, 0, j)),
        out_shape=jax.ShapeDtypeStruct((depth, rows, n), F32),
        compiler_params=_cparams("parallel", "parallel"),
        name="ada_modulation",
    )(cc, w_ada, b_ada.reshape(depth, 1, n))


def _proj_kernel(x_ref, mod_ref, w_ref, *rest, shift, scale, has_gate):
    if has_gate:
        wg_ref, o_ref, og_ref, xs_ref = rest
    else:
        o_ref, xs_ref = rest

    @pl.when(pl.program_id(2) == 0)
    def _():
        h = x_ref[0] * (1.0 + mod_ref[0, scale:scale + 1, :]) + mod_ref[0, shift:shift + 1, :]
        xs_ref[...] = h.astype(BF16)
        if has_gate:
            og_ref[0] = jnp.dot(xs_ref[...], wg_ref[...], preferred_element_type=F32)

    o_ref[0] = jnp.dot(xs_ref[...], w_ref[...], preferred_element_type=F32).astype(o_ref.dtype)


def _proj(x, mod, w, wg=None, *, shift, scale, out_dtype=F32, name):
    g, r, k = x.shape
    n = w.shape[1]
    tm = _pick(r, (1024, 512, 256, 128, 64))
    tn = _pick(n, (1024, 512, 256, 128))
    has_gate = wg is not None
    gm = mod.shape[0]
    in_specs = [pl.BlockSpec((1, tm, k), lambda a, i, j: (a, i, 0)),
                pl.BlockSpec((1, 6, k), lambda a, i, j: (a % gm, 0, 0)),
                pl.BlockSpec((k, tn), lambda a, i, j: (0, j))]
    out_specs = [pl.BlockSpec((1, tm, tn), lambda a, i, j: (a, i, j))]
    out_shape = [jax.ShapeDtypeStruct((g, r, n), out_dtype)]
    args = [x, mod, w]
    if has_gate:
        in_specs.append(pl.BlockSpec((k, HEAD), lambda a, i, j: (0, 0)))
        out_specs.append(pl.BlockSpec((1, tm, HEAD), lambda a, i, j: (a, i, 0)))
        out_shape.append(jax.ShapeDtypeStruct((g, r, HEAD), F32))
        args.append(wg)
    outs = pl.pallas_call(
        functools.partial(_proj_kernel, shift=shift, scale=scale, has_gate=has_gate),
        grid=(g, r // tm, n // tn),
        in_specs=in_specs, out_specs=out_specs, out_shape=out_shape,
        scratch_shapes=[pltpu.VMEM((tm, k), BF16)],
        compiler_params=_cparams("parallel", "parallel", "arbitrary"),
        name=name,
    )(*args)
    return outs if has_gate else outs[0]


def _merge_half(of_ref, ob_ref, gate_ref, norm_ref, ys_ref, col0, nheads):
    for h in range(nheads):
        hs = slice(h * HEAD, (h + 1) * HEAD)
        o = of_ref[0, :, hs] + ob_ref[0, :, hs]
        ms = jnp.mean(o * o, axis=-1, keepdims=True)
        y = o * lax.rsqrt(ms + EPS) * norm_ref[...] * _silu(gate_ref[0, :, hs])
        ys_ref[:, col0 + h * HEAD:col0 + (h + 1) * HEAD] = y.astype(BF16)


def _outproj_kernel(*refs, merge, alpha, a_heads, b_heads):
    if merge:
        (oaf, oab, obf, obb, ga, zb, an, bn, w_ref, res_ref, mod_ref, g_ref, b_ref, o_ref, ys_ref) = refs
        _merge_half(oaf, oab, ga, an, ys_ref, 0, a_heads)
        _merge_half(obf, obb, zb, bn, ys_ref, a_heads * HEAD, b_heads)
        y = ys_ref[...]
    else:
        x_ref, w_ref, res_ref, mod_ref, g_ref, b_ref, o_ref = refs
        y = x_ref[0]
    acc = jnp.dot(y, w_ref[...], preferred_element_type=F32)
    r = alpha * res_ref[0] + mod_ref[0, 2:3, :] * acc
    o_ref[0] = _layernorm_rows(r, g_ref[...], b_ref[...])


def _outproj(xs, w, res, mod, ln_g, ln_b, *, alpha, merge=None, name):
    g, r, d = res.shape
    k = w.shape[0]
    tm = _pick(r, (512, 256, 128, 64) if merge is None else (256, 128, 64))
    gm = mod.shape[0]
    row = lambda a, i: (a, i, 0)
    fixed = lambda a, i: (0, 0)
    if merge is None:
        in_specs = [pl.BlockSpec((1, tm, k), row)]
        args = [xs]
        scratch = []
        kern = functools.partial(_outproj_kernel, merge=False, alpha=alpha, a_heads=0, b_heads=0)
    else:
        oaf, oab, obf, obb, p, an, bn = xs
        a_val, b_val, ga_blk, zb_blk = merge
        in_specs = [pl.BlockSpec((1, tm, a_val), row), pl.BlockSpec((1, tm, a_val), row),
                    pl.BlockSpec((1, tm, b_val), row), pl.BlockSpec((1, tm, b_val), row),
                    pl.BlockSpec((1, tm, a_val), lambda a, i: (a, i, ga_blk)),
                    pl.BlockSpec((1, tm, b_val), lambda a, i: (a, i, zb_blk)),
                    pl.BlockSpec((1, HEAD), fixed), pl.BlockSpec((1, HEAD), fixed)]
        args = [oaf, oab, obf, obb, p, p, an, bn]
        scratch = [pltpu.VMEM((tm, k), BF16)]
        kern = functools.partial(_outproj_kernel, merge=True, alpha=alpha,
                                 a_heads=a_val // HEAD, b_heads=b_val // HEAD)
    in_specs += [pl.BlockSpec((k, d), fixed, pipeline_mode=pl.Buffered(1)),
                 pl.BlockSpec((1, tm, d), row),
                 pl.BlockSpec((1, 6, d), lambda a, i: (a % gm, 0, 0)),
                 pl.BlockSpec((1, d), fixed), pl.BlockSpec((1, d), fixed)]
    args += [w, res, mod, ln_g.reshape(1, d), ln_b.reshape(1, d)]
    return pl.pallas_call(
        kern,
        grid=(g, r // tm),
        in_specs=in_specs,
        out_specs=pl.BlockSpec((1, tm, d), row),
        out_shape=jax.ShapeDtypeStruct((g, r, d), F32),
        scratch_shapes=scratch,
        compiler_params=_cparams("parallel", "parallel"),
        name=name,
    )(*args)


def _tri_masks(n):
    row = lax.broadcasted_iota(jnp.int32, (n, n), 0)
    col = lax.broadcasted_iota(jnp.int32, (n, n), 1)
    return col <= row, col >= row


def _gla_kernel(qf_ref, zf_ref, vf_ref, qb_ref, zb_ref, vb_ref, lb_ref, s0_ref,
                of_ref, ob_ref, sfin_ref, st_ref, *, hb, nchunk):
    t = pl.program_id(2)
    c = A_CHUNK

    @pl.when(t == 0)
    def _():
        st_ref[...] = s0_ref[0]

    causal, anti = _tri_masks(c)
    streams = ((qf_ref, zf_ref, vf_ref, of_ref, causal, c // 2 - 1, c - 1),
               (qb_ref, zb_ref, vb_ref, ob_ref, anti, c - c // 2, 0))

    def body(ci, carry):
        items = []
        for d, (q_ref, z_ref, v_ref, o_ref, mask, mid, last) in enumerate(streams):
            cc = ci if d == 0 else nchunk - 1 - ci
            sl = pl.ds(pl.multiple_of(cc * c, c), c)
            q = _silu(q_ref[0, sl, :])
            v = v_ref[0, sl, :]
            lb = lb_ref[d:d + 1, :]
            f = lb + (1.0 - lb) * _sigmoid(z_ref[0, sl, :])
            k = 1.0 - f
            cum = _tri_dot(mask.astype(BF16), jnp.log(f))
            m = cum[mid:mid + 1, :]
            bl = cum[last:last + 1, :]
            qe = q * jnp.exp(cum)
            qm = q * jnp.exp(cum - m)
            km = k * jnp.exp(m - cum)
            kd = k * jnp.exp(bl - cum)
            dec = jnp.exp(bl)
            for h in range(hb):
                hs = slice(h * HEAD, (h + 1) * HEAD)
                items.append(dict(d=d, h=h, sl=sl, hs=hs, o_ref=o_ref, mask=mask, qe=qe[:, hs], qm=qm[:, hs],
                                  km=km[:, hs], kd=kd[:, hs], v=v[:, hs], dec=dec[:, hs]))
        for it in items:
            it["a"] = jnp.where(it["mask"], _dot_nt(it["qm"], it["km"]), 0.0)
            it["u"] = _dot_tn(it["v"], it["kd"])
        for it in items:
            it["st"] = st_ref[it["d"], it["h"]]
            it["o"] = _dot_nt(it["qe"], it["st"]) + _dot(it["a"], it["v"])
        for it in items:
            it["o_ref"][0, it["sl"], it["hs"]] = it["o"]
            st_ref[it["d"], it["h"]] = it["st"] * it["dec"] + it["u"]
        return carry

    lax.fori_loop(0, nchunk, body, 0)

    @pl.when(t == pl.num_programs(2) - 1)
    def _():
        sfin_ref[0] = st_ref[...]


def _gla(p, lb, s0, *, a_key, name):
    b, l, _ = p.shape
    heads = a_key // HEAD
    hb = _pick(heads, (8, 4, 2, 1))
    wb = hb * HEAD
    per = a_key // wb
    tb = _pick(l, (256, 128, 64, 32))
    nblk = l // tb
    fwd = lambda grp: (lambda bi, hg, t: (bi, t, grp * per + hg))
    bwd = lambda grp: (lambda bi, hg, t: (bi, nblk - 1 - t, grp * per + hg))
    blk = (1, tb, wb)
    st_blk = (1, 2, hb, HEAD, HEAD)
    st_map = lambda bi, hg, t: (bi, 0, hg, 0, 0)
    return pl.pallas_call(
        functools.partial(_gla_kernel, hb=hb, nchunk=tb // A_CHUNK),
        grid=(b, heads // hb, nblk),
        in_specs=[pl.BlockSpec(blk, fwd(0)), pl.BlockSpec(blk, fwd(1)), pl.BlockSpec(blk, fwd(3)),
                  pl.BlockSpec(blk, bwd(0)), pl.BlockSpec(blk, bwd(2)), pl.BlockSpec(blk, bwd(3)),
                  pl.BlockSpec((2, wb), lambda bi, hg, t: (0, hg)),
                  pl.BlockSpec(st_blk, st_map)],
        out_specs=[pl.BlockSpec(blk, lambda bi, hg, t: (bi, t, hg)),
                   pl.BlockSpec(blk, lambda bi, hg, t: (bi, nblk - 1 - t, hg)),
                   pl.BlockSpec(st_blk, st_map)],
        out_shape=[jax.ShapeDtypeStruct((b, l, a_key), F32),
                   jax.ShapeDtypeStruct((b, l, a_key), F32),
                   jax.ShapeDtypeStruct(s0.shape, F32)],
        scratch_shapes=[pltpu.VMEM((2, hb, HEAD, HEAD), F32)],
        compiler_params=_cparams("parallel", "parallel", "arbitrary"),
        name=name,
    )(p, p, p, p, p, p, lb, s0)


def _gdn_prep_kernel(x_ref, w_ref, o_ref, *, qk_blocks, q_blocks, width):
    x = x_ref[0]
    l = x.shape[0]
    pos = lax.broadcasted_iota(jnp.int32, x.shape, 0)
    pad = width // 2
    acc = x * w_ref[pad:pad + 1, :]
    for j in range(width):
        s = j - pad
        if s == 0:
            continue
        shifted = pltpu.roll(x, (-s) % l, 0)
        ok = (pos + s >= 0) & (pos + s < l)
        acc = acc + jnp.where(ok, shifted, 0.0) * w_ref[j:j + 1, :]
    y = _silu(acc)
    cb = pl.program_id(1)
    inv = lax.rsqrt(jnp.sum(y * y, axis=-1, keepdims=True) + EPS)
    inv = jnp.where(cb < qk_blocks, inv, 1.0) * jnp.where(cb < q_blocks, HEAD ** -0.5, 1.0)
    o_ref[0] = y * inv


def _gdn_prep(p, conv_w, *, col0, b_key, name):
    b, l, _ = p.shape
    width, c = conv_w.shape
    nb = c // HEAD
    off = col0 // HEAD
    return pl.pallas_call(
        functools.partial(_gdn_prep_kernel, qk_blocks=2 * b_key // HEAD, q_blocks=b_key // HEAD, width=width),
        grid=(b, nb),
        in_specs=[pl.BlockSpec((1, l, HEAD), lambda bi, cb: (bi, 0, off + cb)),
                  pl.BlockSpec((width, HEAD), lambda bi, cb: (0, cb))],
        out_specs=pl.BlockSpec((1, l, HEAD), lambda bi, cb: (bi, 0, cb)),
        out_shape=jax.ShapeDtypeStruct((b, l, c), F32),
        compiler_params=_cparams("parallel", "parallel"),
        name=name,
    )(p, conv_w)


def _gdn_kernel(qf_ref, kf_ref, vf_ref, gcf_ref, grf_ref, qb_ref, kb_ref, vb_ref, gcb_ref, grb_ref,
                pc_ref, pr_ref, s0_ref, of_ref, ob_ref, sfin_ref, st_ref, *, hb, nchunk):
    t = pl.program_id(2)
    c = B_CHUNK

    @pl.when(t == 0)
    def _():
        st_ref[...] = s0_ref[0]

    row = lax.broadcasted_iota(jnp.int32, (c, HEAD), 0)
    col = lax.broadcasted_iota(jnp.int32, (c, HEAD), 1)
    right = col >= c
    eye_right = (col == row + c).astype(F32)
    causal, anti = _tri_masks(c)
    zeros_k = jnp.zeros((c, HEAD), F32)
    dirs = ((col <= row, col < row, causal.astype(BF16), ((col >= row) & ~right).astype(BF16), c - 1),
            ((col >= row) & ~right, (col > row) & ~right, anti.astype(BF16), (col <= row).astype(BF16), 0))
    streams = ((qf_ref, kf_ref, vf_ref, gcf_ref, grf_ref, of_ref), (qb_ref, kb_ref, vb_ref, gcb_ref, grb_ref, ob_ref))

    def body(ci, carry):
        items = []
        for d in range(2):
            q_ref, k_ref, v_ref, gc_ref, gr_ref, o_ref = streams[d]
            incl, strict, tri_c, tri_r, last = dirs[d]
            cc = ci if d == 0 else nchunk - 1 - ci
            sl = pl.ds(pl.multiple_of(cc * c, c), c)
            gcol = gc_ref[0, 0, sl, :]
            a_log = pr_ref[0, d:d + 1, 0:hb]
            dt = pr_ref[0, 2 + d:3 + d, 0:hb]
            g_col = _tri_dot(tri_c, -jnp.exp(a_log) * _softplus(gcol[:, d * hb:(d + 1) * hb] + dt))
            beta_col = _sigmoid(gcol[:, (2 + d) * hb:(3 + d) * hb])
            grow = gr_ref[0, 0, cc]
            a_log_c = pc_ref[0, :, d:d + 1]
            dt_c = pc_ref[0, :, 2 + d:3 + d]
            g_row = _dot_tri(-jnp.exp(a_log_c) * _softplus(grow[d * hb:(d + 1) * hb, :] + dt_c), tri_r)
            for hl in range(hb):
                hs = slice(hl * HEAD, (hl + 1) * HEAD)
                items.append(dict(d=d, hl=hl, sl=sl, hs=hs, o_ref=o_ref, incl=incl, strict=strict, last=last,
                                  gc=g_col[:, hl:hl + 1], bc=beta_col[:, hl:hl + 1], gr=g_row[hl:hl + 1, :],
                                  q=q_ref[0, sl, hs], k=k_ref[0, sl, hs], v=v_ref[0, sl, hs]))
        for it in items:
            it["decay"] = jnp.exp(jnp.where(it["incl"], it["gc"] - it["gr"], NEG))
            it["kb"] = it["k"] * it["bc"]
        for it in items:
            x = jnp.concatenate([it["kb"], it["q"]], axis=0)
            kpad = jnp.concatenate([it["k"], zeros_k], axis=0)
            it["kq"] = _dot_nt(x, kpad)
        for it in items:
            it["w"] = eye_right - jnp.where(it["strict"], it["kq"][:c] * it["decay"], 0.0)
            it["attn"] = it["kq"][c:] * it["decay"]
        for _ in range(int(np.log2(c))):
            rs = [_dot3(it["w"][:, :c], it["w"]) for it in items]
            for it, r in zip(items, rs):
                it["w"] = r + jnp.where(right, it["w"], 0.0)
        for it in items:
            tinv = pltpu.roll(it["w"], c, 1)[:, :c]
            egc = jnp.exp(it["gc"])
            rhs = jnp.concatenate([it["v"] * it["bc"], it["kb"] * egc], axis=1)
            it["sol"] = _dot3(tinv, rhs)
            g_last = it["gc"][it["last"]:it["last"] + 1, :]
            it["qe"] = it["q"] * egc
            it["kdt"] = jnp.transpose(it["k"] * jnp.exp(g_last - it["gc"]))
            it["dlast"] = jnp.exp(g_last)
        for it in items:
            it["s"] = st_ref[it["d"], it["hl"]]
            it["r"] = _dot(jnp.concatenate([it["sol"][:, HEAD:], it["qe"]], axis=0), it["s"])
        for it in items:
            it["vn"] = it["sol"][:, :HEAD] - it["r"][:c]
            it["r2"] = _dot(jnp.concatenate([it["attn"][:, :c], it["kdt"]], axis=0), it["vn"])
        for it in items:
            it["o_ref"][0, it["sl"], it["hs"]] = it["r"][c:] + it["r2"][:c]
            st_ref[it["d"], it["hl"]] = it["s"] * it["dlast"] + it["r2"][c:]
        return carry

    lax.fori_loop(0, nchunk, body, 0)

    @pl.when(t == pl.num_programs(2) - 1)
    def _():
        sfin_ref[0] = st_ref[...]


def _gdn(qkv, gates, prm, s0, *, b_key, name):
    b, l, _ = qkv.shape
    assert HEAD == 2 * B_CHUNK
    heads = b_key // HEAD
    hb = _pick(heads, (8, 4, 2, 1))
    nhg = heads // hb
    wb = hb * HEAD
    tb = _pick(l, (256, 128, 64))
    nblk = l // tb
    cpb = tb // B_CHUNK
    g4 = gates[:, :, :4 * heads].reshape(b, l, 4, nhg, hb)
    gcol = jnp.transpose(g4, (0, 3, 1, 2, 4)).reshape(b, nhg, l, 4 * hb)
    grow = jnp.swapaxes(gcol.reshape(b, nhg, l // B_CHUNK, B_CHUNK, 4 * hb), 3, 4)
    gcol = jnp.pad(gcol, ((0, 0), (0, 0), (0, 0), (0, HEAD - 4 * hb)))
    p4 = prm.reshape(4, nhg, hb)
    prow = jnp.pad(jnp.transpose(p4, (1, 0, 2)), ((0, 0), (0, 0), (0, HEAD - hb)))
    pcol = jnp.pad(jnp.transpose(p4, (1, 2, 0)), ((0, 0), (0, 0), (0, HEAD - 4)))
    fwd = lambda grp: (lambda bi, hg, t: (bi, t, grp * nhg + hg))
    bwd = lambda grp: (lambda bi, hg, t: (bi, nblk - 1 - t, grp * nhg + hg))
    blk = (1, tb, wb)
    gc_blk = (1, 1, tb, HEAD)
    gr_blk = (1, 1, cpb, 4 * hb, B_CHUNK)
    st_blk = (1, 2, hb, HEAD, HEAD)
    st_map = lambda bi, hg, t: (bi, 0, hg, 0, 0)
    return pl.pallas_call(
        functools.partial(_gdn_kernel, hb=hb, nchunk=cpb),
        grid=(b, nhg, nblk),
        in_specs=[pl.BlockSpec(blk, fwd(0)), pl.BlockSpec(blk, fwd(1)), pl.BlockSpec(blk, fwd(2)),
                  pl.BlockSpec(gc_blk, lambda bi, hg, t: (bi, hg, t, 0)),
                  pl.BlockSpec(gr_blk, lambda bi, hg, t: (bi, hg, t, 0, 0)),
                  pl.BlockSpec(blk, bwd(0)), pl.BlockSpec(blk, bwd(1)), pl.BlockSpec(blk, bwd(2)),
                  pl.BlockSpec(gc_blk, lambda bi, hg, t: (bi, hg, nblk - 1 - t, 0)),
                  pl.BlockSpec(gr_blk, lambda bi, hg, t: (bi, hg, nblk - 1 - t, 0, 0)),
                  pl.BlockSpec((1, hb, HEAD), lambda bi, hg, t: (hg, 0, 0)),
                  pl.BlockSpec((1, 4, HEAD), lambda bi, hg, t: (hg, 0, 0)),
                  pl.BlockSpec(st_blk, st_map)],
        out_specs=[pl.BlockSpec(blk, lambda bi, hg, t: (bi, t, hg)),
                   pl.BlockSpec(blk, lambda bi, hg, t: (bi, nblk - 1 - t, hg)),
                   pl.BlockSpec(st_blk, st_map)],
        out_shape=[jax.ShapeDtypeStruct((b, l, b_key), F32),
                   jax.ShapeDtypeStruct((b, l, b_key), F32),
                   jax.ShapeDtypeStruct(s0.shape, F32)],
        scratch_shapes=[pltpu.VMEM((2, hb, HEAD, HEAD), F32)],
        compiler_params=_cparams("parallel", "parallel", "arbitrary"),
        name=name,
    )(qkv, qkv, qkv, gcol, grow, qkv, qkv, qkv, gcol, grow, pcol, prow, s0)


def _rope(x, cos, s1, s2):
    q = HEAD // 4
    return x * cos + pltpu.roll(x, HEAD - q, 1) * s1 + pltpu.roll(x, q, 1) * s2


def _rope_tables(l):
    quarter = HEAD // 4
    pos = jnp.arange(l)
    inv = ROPE_BASE ** (-jnp.arange(quarter, dtype=F32) / quarter)
    zero = jnp.zeros((l, quarter), F32)
    cos, s1, s2 = [], [], []
    for p in ((pos // GRID_W).astype(F32), (pos % GRID_W).astype(F32)):
        ang = p[:, None] * inv[None, :]
        cos += [jnp.cos(ang), jnp.cos(ang)]
        s1 += [-jnp.sin(ang), zero]
        s2 += [zero, jnp.sin(ang)]
    return (jnp.concatenate(cos, -1), jnp.concatenate(s1, -1), jnp.concatenate(s2, -1))


def _nattn_plan(rows):
    wr = min(WIN_R, rows)
    tr = min(4, rows)
    kbr = min(rows, tr + wr)
    qc = np.arange(GRID_W)
    c0 = np.clip(qc - WIN_C // 2, 0, GRID_W - WIN_C)
    ok_c = (qc[None, :] >= c0[:, None]) & (qc[None, :] < c0[:, None] + WIN_C)
    dc = np.clip(qc[None, :] - qc[:, None] + WIN_C - 1, 0, 2 * WIN_C - 2)
    tiles, classes, keys = [], [], {}
    for t in range(rows // tr):
        ks = int(np.clip(tr * t - wr // 2, 0, rows - kbr))
        qr = tr * t + np.arange(tr)
        kr = ks + np.arange(kbr)
        r0 = np.clip(qr - wr // 2, 0, rows - wr)
        ok_r = (kr[None, :] >= r0[:, None]) & (kr[None, :] < r0[:, None] + wr)
        dr = np.clip(kr[None, :] - qr[:, None] + WIN_R - 1, 0, 2 * WIN_R - 2)
        valid = (ok_r[:, None, :, None] & ok_c[None, :, None, :]).reshape(tr * GRID_W, kbr * GRID_W)
        key = dr.tobytes() + valid.tobytes()
        if key not in keys:
            keys[key] = len(classes)
            classes.append((dr, valid))
        tiles.append((tr * t * GRID_W, ks * GRID_W, keys[key]))
    dr = np.stack([cl[0] for cl in classes]).astype(np.int32)
    valid = np.stack([cl[1] for cl in classes])
    return tuple(tiles), dr, dc.astype(np.int32), valid


def _nattn_bias(rel_bias, dr, dc, valid):
    h = rel_bias.shape[0]
    ncls, tr, kbr = dr.shape
    w = dc.shape[0]
    by_col = jnp.take(rel_bias, dc.reshape(-1), axis=2).reshape(h, -1, w, w)
    tab = jnp.take(by_col, dr.reshape(-1), axis=1).reshape(h, ncls, tr, kbr, w, w)
    tab = jnp.transpose(tab, (0, 1, 2, 4, 3, 5)).reshape(h, ncls, tr * w, kbr * w)
    return jnp.where(valid[None], tab, NEG)


def _nattn_kernel(q_ref, k_ref, v_ref, kc_ref, vc_ref, bias_ref, cos_ref, s1_ref, s2_ref,
                  o_ref, kr_ref, vr_ref, *, tiles):
    tq, tk = bias_ref.shape[2], bias_ref.shape[3]
    kr_ref[...] = _rope(k_ref[0], cos_ref[...], s1_ref[...], s2_ref[...]).astype(BF16)
    vr_ref[...] = v_ref[0].astype(BF16)
    kc = kc_ref[0].astype(BF16)
    vc = vc_ref[0].astype(BF16)
    for q0, k0, cls in tiles:
        qs = slice(q0, q0 + tq)
        qt = _rope(q_ref[0, qs, :], cos_ref[qs, :], s1_ref[qs, :], s2_ref[qs, :]) * HEAD ** -0.5
        qt = qt.astype(BF16)
        s_lat = _dot_nt(qt, kr_ref[k0:k0 + tk, :]) + bias_ref[0, cls]
        s_ctx = _dot_nt(qt, kc)
        m = jnp.maximum(jnp.max(s_lat, axis=-1, keepdims=True), jnp.max(s_ctx, axis=-1, keepdims=True))
        p_lat = jnp.exp(s_lat - m)
        p_ctx = jnp.exp(s_ctx - m)
        den = jnp.sum(p_lat, axis=-1, keepdims=True) + jnp.sum(p_ctx, axis=-1, keepdims=True)
        o = _dot(p_lat, vr_ref[k0:k0 + tk, :]) + _dot(p_ctx, vc)
        o_ref[0, qs, :] = (o / den).astype(o_ref.dtype)


def _nattn(p_lat, p_ctx, bias, tiles, ropes, *, heads, kc_blk, vc_blk, name):
    b, l, _ = p_lat.shape
    lc = p_ctx.shape[1]
    ncls, tq, tk = bias.shape[1:]
    tab = pl.BlockSpec((l, HEAD), lambda h, bi: (0, 0))
    return pl.pallas_call(
        functools.partial(_nattn_kernel, tiles=tiles),
        grid=(heads, b),
        in_specs=[pl.BlockSpec((1, l, HEAD), lambda h, bi: (bi, 0, h)),
                  pl.BlockSpec((1, l, HEAD), lambda h, bi: (bi, 0, heads + h)),
                  pl.BlockSpec((1, l, HEAD), lambda h, bi: (bi, 0, 2 * heads + h)),
                  pl.BlockSpec((1, lc, HEAD), lambda h, bi: (bi, 0, kc_blk + h)),
                  pl.BlockSpec((1, lc, HEAD), lambda h, bi: (bi, 0, vc_blk + h)),
                  pl.BlockSpec((1, ncls, tq, tk), lambda h, bi: (h, 0, 0, 0)),
                  tab, tab, tab],
        out_specs=pl.BlockSpec((1, l, HEAD), lambda h, bi: (bi, 0, h)),
        out_shape=jax.ShapeDtypeStruct((b, l, heads * HEAD), BF16),
        scratch_shapes=[pltpu.VMEM((l, HEAD), BF16), pltpu.VMEM((l, HEAD), BF16)],
        compiler_params=_cparams("parallel", "parallel"),
        name=name,
    )(p_lat, p_lat, p_lat, p_ctx, p_ctx, bias, *ropes)


def _cattn_kernel(q_ref, k_ref, v_ref, o_ref):
    s = _dot_nt(q_ref[0] * HEAD ** -0.5, k_ref[0])
    p = jnp.exp(s - jnp.max(s, axis=-1, keepdims=True))
    o = _dot(p, v_ref[0]) / jnp.sum(p, axis=-1, keepdims=True)
    o_ref[0] = o.astype(o_ref.dtype)


def _cattn(p_ctx, *, heads, name):
    b, lc, _ = p_ctx.shape
    blk = (1, lc, HEAD)
    return pl.pallas_call(
        _cattn_kernel,
        grid=(b, heads),
        in_specs=[pl.BlockSpec(blk, lambda bi, h: (bi, 0, h)),
                  pl.BlockSpec(blk, lambda bi, h: (bi, 0, heads + h)),
                  pl.BlockSpec(blk, lambda bi, h: (bi, 0, 2 * heads + h))],
        out_specs=pl.BlockSpec(blk, lambda bi, h: (bi, 0, h)),
        out_shape=jax.ShapeDtypeStruct((b, lc, heads * HEAD), BF16),
        compiler_params=_cparams("parallel", "parallel"),
        name=name,
    )(p_ctx, p_ctx, p_ctx)


def _ffn_up_kernel(x_ref, xp_ref, xn_ref, mod_ref, wa_ref, wg_ref, wdw_ref, bdw_ref, o_ref, xs_ref):
    tm = x_ref.shape[1]
    i = pl.program_id(1)

    @pl.when(pl.program_id(2) == 0)
    def _():
        sc = 1.0 + mod_ref[0, 4:5, :]
        sh = mod_ref[0, 3:4, :]
        xs_ref[0:HALO, :] = (xp_ref[0] * sc + sh).astype(BF16)
        xs_ref[HALO:HALO + tm, :] = (x_ref[0] * sc + sh).astype(BF16)
        xs_ref[HALO + tm:, :] = (xn_ref[0] * sc + sh).astype(BF16)

    a = jnp.dot(xs_ref[...], wa_ref[...], preferred_element_type=F32)
    gate = jnp.dot(xs_ref[HALO:HALO + tm, :], wg_ref[...], preferred_element_type=F32)
    ext = tm + 2 * HALO
    a_prev = pltpu.roll(a, 1, 0)[HALO:HALO + tm]
    a_next = pltpu.roll(a, ext - 1, 0)[HALO:HALO + tm]
    pos = lax.broadcasted_iota(jnp.int32, gate.shape, 0)
    a_prev = jnp.where((pos == 0) & (i == 0), 0.0, a_prev)
    a_next = jnp.where((pos == tm - 1) & (i == pl.num_programs(1) - 1), 0.0, a_next)
    conv = (a_prev * wdw_ref[0:1, :] + a[HALO:HALO + tm] * wdw_ref[1:2, :] + a_next * wdw_ref[2:3, :]
            + bdw_ref[...])
    gelu = 0.5 * conv * (1.0 + lax.erf(conv * np.float32(np.sqrt(0.5))))
    o_ref[0] = (gelu * gate).astype(o_ref.dtype)


def _ffn_up(x, mod, wa, wg, wdw, bdw, *, name):
    g, r, k = x.shape
    f = wa.shape[1]
    tm = _pick(r, (1024, 512, 256, 128, 64, 32, 16))
    tn = _pick(f, (512, 256, 128))
    hpt = tm // HALO
    nh = r // HALO
    gm = mod.shape[0]
    return pl.pallas_call(
        _ffn_up_kernel,
        grid=(g, r // tm, f // tn),
        in_specs=[pl.BlockSpec((1, tm, k), lambda a, i, j: (a, i, 0)),
                  pl.BlockSpec((1, HALO, k), lambda a, i, j: (a, jnp.maximum(i * hpt - 1, 0), 0)),
                  pl.BlockSpec((1, HALO, k), lambda a, i, j: (a, jnp.minimum((i + 1) * hpt, nh - 1), 0)),
                  pl.BlockSpec((1, 6, k), lambda a, i, j: (a % gm, 0, 0)),
                  pl.BlockSpec((k, tn), lambda a, i, j: (0, j)),
                  pl.BlockSpec((k, tn), lambda a, i, j: (0, j)),
                  pl.BlockSpec((3, tn), lambda a, i, j: (0, j)),
                  pl.BlockSpec((1, tn), lambda a, i, j: (0, j))],
        out_specs=pl.BlockSpec((1, tm, tn), lambda a, i, j: (a, i, j)),
        out_shape=jax.ShapeDtypeStruct((g, r, f), BF16),
        scratch_shapes=[pltpu.VMEM((tm + 2 * HALO, k), BF16)],
        compiler_params=_cparams("parallel", "parallel", "arbitrary"),
        name=name,
    )(x, x, x, mod, wa, wg, wdw, bdw)


def _ffn_down_kernel(x_ref, w_ref, res_ref, mod_ref, g_ref, b_ref, o_ref, *, alpha):
    acc = jnp.dot(x_ref[0], w_ref[...], preferred_element_type=F32)
    r = alpha * res_ref[0] + mod_ref[0, 5:6, :] * acc
    o_ref[0] = _layernorm_rows(r, g_ref[...], b_ref[...])


def _ffn_down(x, w, res, mod, ln_g, ln_b, *, alpha, name):
    g, r, f = x.shape
    d = w.shape[1]
    tm = _pick(r, (256, 128, 64))
    gm = mod.shape[0]
    row = lambda a, i: (a, i, 0)
    fixed = lambda a, i: (0, 0)
    return pl.pallas_call(
        functools.partial(_ffn_down_kernel, alpha=alpha),
        grid=(g, r // tm),
        in_specs=[pl.BlockSpec((1, tm, f), row),
                  pl.BlockSpec((f, d), fixed, pipeline_mode=pl.Buffered(1)),
                  pl.BlockSpec((1, tm, d), row),
                  pl.BlockSpec((1, 6, d), lambda a, i: (a % gm, 0, 0)),
                  pl.BlockSpec((1, d), fixed), pl.BlockSpec((1, d), fixed)],
        out_specs=pl.BlockSpec((1, tm, d), row),
        out_shape=jax.ShapeDtypeStruct((g, r, d), F32),
        compiler_params=_cparams("parallel", "parallel"),
        name=name,
    )(x, w, res, mod, ln_g.reshape(1, d), ln_b.reshape(1, d))


def _pad_cols(w, n):
    return jnp.pad(w, ((0, 0), (0, n - w.shape[1])))


def kernel(x, c, ctx, c_ctx, w_ada, b_ada, ln_g, ln_b, w_in_ab, hgrn_lb, hgrn_norm, gdn_conv, gdn_a_log,
           gdn_dt_bias, gdn_norm, w_out_ab, w_in_c, na_rel_bias, w_out_c, ffn_w_up, ffn_w_dw, ffn_b_dw,
           ffn_w_down):
    b, l, d = x.shape
    lc = ctx.shape[1]
    depth = w_ada.shape[0]
    alpha = (2 * depth) ** 0.25
    a_key = hgrn_lb.shape[-1]
    b_heads = gdn_a_log.shape[-1]
    b_key = b_heads * HEAD
    b_val = gdn_conv.shape[-1] - 2 * b_key
    a_val = w_out_ab.shape[1] - b_val
    n_main = 3 * a_key + 2 * a_val + 2 * b_key + 2 * b_val
    c_heads = d // HEAD
    d_ff = ffn_w_down.shape[1]
    ffp = -(-d_ff // FF_ALIGN) * FF_ALIGN
    assert a_key == a_val == b_key == b_val and w_in_ab.shape[-1] == n_main + 4 * b_heads
    assert l % (4 * GRID_W) == 0 or l // GRID_W < 4

    rows = -(-(b + 1) // 8) * 8
    cc = jnp.concatenate([c, c_ctx[None, :], jnp.zeros((rows - b - 1, d), F32)], axis=0)
    mod = _ada(cc, w_ada, b_ada).reshape(depth, rows, 6, d)

    lb_all = jax.nn.softmax(hgrn_lb.astype(F32), axis=0)
    lb_all = jnp.cumsum(lb_all, axis=0) - lb_all[0]

    tiles, bias_dr, bias_dc, bias_ok = _nattn_plan(l // GRID_W)
    ropes = _rope_tables(l)
    s_zero = jnp.zeros((b, 2, a_key // HEAD, HEAD, HEAD), F32)

    xl, xc = x, ctx
    for layer in range(depth):
        need_ctx = layer < depth - 1
        mod_l = mod[layer, :b]
        mod_c = mod[layer, b:b + 1]
        g0, b0 = ln_g[layer, 0], ln_b[layer, 0]
        if layer % 2 == 0:
            e = layer // 2
            w_main = w_in_ab[e, :, :n_main].astype(BF16)
            w_gate = _pad_cols(w_in_ab[e, :, n_main:], HEAD).astype(BF16)
            w_out = w_out_ab[e].astype(BF16)
            an = hgrn_norm[e].reshape(1, HEAD)
            bn = gdn_norm[e].reshape(1, HEAD)
            gdn_prm = jnp.concatenate([gdn_a_log[e], gdn_dt_bias[e]], axis=0)
            merge = (a_val, b_val, (3 * a_key + a_val) // a_val, (n_main - b_val) // b_val)
            sa, sb = s_zero, s_zero
            for is_ctx, xs, mods in ((True, xc, mod_c), (False, xl, mod_l)):
                tag = f"l{layer}_{'ctx' if is_ctx else 'lat'}"
                p, gates = _proj(xs, mods, w_main, w_gate, shift=0, scale=1,
                                 name=f"inproj_ab_{tag}")
                oaf, oab, sa = _gla(p, lb_all[e], sa, a_key=a_key, name=f"hgrn2_{tag}")
                qkv = _gdn_prep(p, gdn_conv[e], col0=3 * a_key + 2 * a_val, b_key=b_key, name=f"gdn_prep_{tag}")
                obf, obb, sb = _gdn(qkv, gates, gdn_prm, sb, b_key=b_key, name=f"gdn_{tag}")
                if is_ctx and not need_ctx:
                    continue
                y = _outproj((oaf, oab, obf, obb, p, an, bn), w_out, xs, mods,
                             g0, b0, alpha=alpha, merge=merge, name=f"outproj_ab_{tag}")
                if is_ctx:
                    xc = y
                else:
                    xl = y
        else:
            o = layer // 2
            w_in = w_in_c[o].astype(BF16)
            w_out = w_out_c[o].astype(BF16)
            bias = _nattn_bias(na_rel_bias[o], bias_dr, bias_dc, bias_ok)
            p_lat = _proj(xl, mod_l, w_in, shift=0, scale=1, name=f"inproj_c_l{layer}_lat")
            if need_ctx:
                p_ctx = _proj(xc, mod_c, w_in, shift=0, scale=1, name=f"inproj_c_l{layer}_ctx")
                kc_blk, vc_blk = c_heads, 2 * c_heads
            else:
                p_ctx = _proj(xc, mod_c, w_in[:, d:], shift=0, scale=1, name=f"inproj_c_l{layer}_ctx")
                kc_blk, vc_blk = 0, c_heads
            o_lat = _nattn(p_lat, p_ctx, bias, tiles, ropes, heads=c_heads, kc_blk=kc_blk, vc_blk=vc_blk,
                           name=f"nattn_l{layer}")
            xl = _outproj(o_lat, w_out, xl, mod_l, g0, b0, alpha=alpha, name=f"outproj_c_l{layer}_lat")
            if need_ctx:
                o_ctx = _cattn(p_ctx, heads=c_heads, name=f"cattn_l{layer}")
                xc = _outproj(o_ctx, w_out, xc, mod_c, g0, b0, alpha=alpha, name=f"outproj_c_l{layer}_ctx")

        wa = _pad_cols(ffn_w_up[layer, :, :d_ff], ffp).astype(BF16)
        wg = _pad_cols(ffn_w_up[layer, :, d_ff:], ffp).astype(BF16)
        wdw = _pad_cols(ffn_w_dw[layer], ffp)
        bdw = _pad_cols(ffn_b_dw[layer].reshape(1, d_ff), ffp)
        wdn = jnp.pad(ffn_w_down[layer], ((0, ffp - d_ff), (0, 0))).astype(BF16)
        g1, b1 = ln_g[layer, 1], ln_b[layer, 1]
        for is_ctx, xs, mods in ((True, xc, mod_c), (False, xl, mod_l)):
            if is_ctx and not need_ctx:
                continue
            tag = f"l{layer}_{'ctx' if is_ctx else 'lat'}"
            hmid = _ffn_up(xs, mods, wa, wg, wdw, bdw, name=f"ffn_up_{tag}")
            y = _ffn_down(hmid, wdn, xs, mods, g1, b1, alpha=alpha, name=f"ffn_down_{tag}")
            if is_ctx:
                xc = y
            else:
                xl = y
    return xl
```

```python
import functools

import numpy as np
import jax
import jax.numpy as jnp
from jax import lax
from jax.experimental import pallas as pl
from jax.experimental.pallas import tpu as pltpu

F32 = jnp.float32
BF16 = jnp.bfloat16

HEAD = 128
GRID_W = 64
WIN_R = 8
WIN_C = 16
ROPE_BASE = 10000.0
A_CHUNK = 32
B_CHUNK = 64
EPS = 1e-6
NEG = -1e30
HALO = 16
FF_ALIGN = 512
VMEM_LIMIT = 56 * 1024 * 1024


def _cparams(*sem):
    return pltpu.CompilerParams(dimension_semantics=sem, vmem_limit_bytes=VMEM_LIMIT)


def _pick(n, prefs):
    for p in prefs:
        if n % p == 0:
            return p
    return n


def _sigmoid(x):
    return 1.0 / (1.0 + jnp.exp(-x))


def _silu(x):
    return x * _sigmoid(x)


def _softplus(x):
    return jnp.maximum(x, 0.0) + jnp.log(1.0 + jnp.exp(-jnp.abs(x)))


def _dot(a, b):
    return jnp.dot(a.astype(BF16), b.astype(BF16), preferred_element_type=F32)


def _dot_nt(a, b):
    return lax.dot_general(a.astype(BF16), b.astype(BF16), (((1,), (1,)), ((), ())),
                           preferred_element_type=F32)


def _dot_tn(a, b):
    return lax.dot_general(a.astype(BF16), b.astype(BF16), (((0,), (0,)), ((), ())),
                           preferred_element_type=F32)


def _split2(x):
    hi = x.astype(BF16)
    return hi, (x - hi.astype(F32)).astype(BF16)


def _mm(a, b):
    return jnp.dot(a, b, preferred_element_type=F32)


def _dot3(a, b):
    ah, al = _split2(a)
    bh, bl = _split2(b)
    return _mm(ah, bh) + _mm(ah, bl) + _mm(al, bh)


def _split3(x):
    x1 = x.astype(BF16)
    r = x - x1.astype(F32)
    x2 = r.astype(BF16)
    return x1, x2, (r - x2.astype(F32)).astype(BF16)


def _tri_dot(tri, x):
    return sum(_mm(tri, xi) for xi in _split3(x))


def _dot_tri(x, tri):
    return sum(_mm(xi, tri) for xi in _split3(x))


def _layernorm_rows(r, g, b):
    mu = jnp.mean(r, axis=-1, keepdims=True)
    rc = r - mu
    var = jnp.mean(rc * rc, axis=-1, keepdims=True)
    return rc * lax.rsqrt(var + EPS) * g + b


def _ada_kernel(c_ref, w_ref, b_ref, o_ref):
    s = _silu(c_ref[...])
    o_ref[0] = _dot(s, w_ref[0]) + b_ref[0]


def _ada(cc, w_ada, b_ada):
    depth, d, n = w_ada.shape
    rows = cc.shape[0]
    tn = _pick(n, (1024, 512, 256, 128))
    return pl.pallas_call(
        _ada_kernel,
        grid=(depth, n // tn),
        in_specs=[pl.BlockSpec((rows, d), lambda l, j: (0, 0)),
                  pl.BlockSpec((1, d, tn), lambda l, j: (l, 0, j)),
                  pl.BlockSpec((1, 1, tn), lambda l, j: (l, 0, j))],
        out_specs=pl.BlockSpec((1, rows, tn), lambda l, j: (l, 0, j)),
        out_shape=jax.ShapeDtypeStruct((depth, rows, n), F32),
        compiler_params=_cparams("parallel", "parallel"),
        name="ada_modulation",
    )(cc, w_ada, b_ada.reshape(depth, 1, n))


def _proj_kernel(x_ref, mod_ref, w_ref, o_ref, xs_ref, *, shift, scale):
    @pl.when(pl.program_id(2) == 0)
    def _():
        h = x_ref[0] * (1.0 + mod_ref[0, scale:scale + 1, :]) + mod_ref[0, shift:shift + 1, :]
        xs_ref[...] = h.astype(BF16)

    o_ref[0] = jnp.dot(xs_ref[...], w_ref[...], preferred_element_type=F32).astype(o_ref.dtype)


def _proj(x, mod, w, *, shift, scale, out_dtype=F32, name):
    g, r, k = x.shape
    n = w.shape[1]
    tm = _pick(r, (1024, 512, 256, 128, 64))
    tn = _pick(n, (1024, 512, 256, 128))
    gm = mod.shape[0]
    return pl.pallas_call(
        functools.partial(_proj_kernel, shift=shift, scale=scale),
        grid=(g, r // tm, n // tn),
        in_specs=[pl.BlockSpec((1, tm, k), lambda a, i, j: (a, i, 0)),
                  pl.BlockSpec((1, 6, k), lambda a, i, j: (a % gm, 0, 0)),
                  pl.BlockSpec((k, tn), lambda a, i, j: (0, j))],
        out_specs=pl.BlockSpec((1, tm, tn), lambda a, i, j: (a, i, j)),
        out_shape=jax.ShapeDtypeStruct((g, r, n), out_dtype),
        scratch_shapes=[pltpu.VMEM((tm, k), BF16)],
        compiler_params=_cparams("parallel", "parallel", "arbitrary"),
        name=name,
    )(x, mod, w)


def _seq_pos(i, tm, shape, seq_len):
    return (i * tm + lax.broadcasted_iota(jnp.int32, shape, 0)) % seq_len


def _proj_ab_kernel(x_ref, xp_ref, xn_ref, mod_ref, w_ref, wg_ref, cw_ref, o_ref, og_ref, xs_ref, *,
                    seq_len, conv0, nconv, qk_heads, q_heads, width):
    tm, tn = o_ref.shape[1], o_ref.shape[2]
    i = pl.program_id(1)
    j = pl.program_id(2)

    @pl.when(j == 0)
    def _():
        sc = 1.0 + mod_ref[0, 1:2, :]
        sh = mod_ref[0, 0:1, :]
        xs_ref[0:HALO, :] = (xp_ref[0] * sc + sh).astype(BF16)
        xs_ref[HALO:HALO + tm, :] = (x_ref[0] * sc + sh).astype(BF16)
        xs_ref[HALO + tm:, :] = (xn_ref[0] * sc + sh).astype(BF16)
        og_ref[0] = jnp.dot(xs_ref[HALO:HALO + tm, :], wg_ref[...], preferred_element_type=F32)

    is_conv = (j >= conv0) & (j < conv0 + nconv)

    @pl.when(jnp.logical_not(is_conv))
    def _():
        o_ref[0] = jnp.dot(xs_ref[HALO:HALO + tm, :], w_ref[...], preferred_element_type=F32)

    @pl.when(is_conv)
    def _():
        a = jnp.dot(xs_ref[...], w_ref[...], preferred_element_type=F32)
        ext = tm + 2 * HALO
        pos = _seq_pos(i, tm, (tm, tn), seq_len)
        pad = width // 2
        acc = a[HALO:HALO + tm] * cw_ref[pad:pad + 1, :]
        for t in range(width):
            s = t - pad
            if s == 0:
                continue
            shifted = pltpu.roll(a, (-s) % ext, 0)[HALO:HALO + tm]
            ok = (pos + s >= 0) & (pos + s < seq_len)
            acc = acc + jnp.where(ok, shifted, 0.0) * cw_ref[t:t + 1, :]
        y = _silu(acc)
        hpt = tn // HEAD
        for h in range(hpt):
            hs = slice(h * HEAD, (h + 1) * HEAD)
            head = (j - conv0) * hpt + h
            yh = y[:, hs]
            inv = lax.rsqrt(jnp.sum(yh * yh, axis=-1, keepdims=True) + EPS)
            gain = jnp.where(head < qk_heads, inv, 1.0) * jnp.where(head < q_heads, HEAD ** -0.5, 1.0)
            o_ref[0, :, hs] = yh * gain


def _proj_ab(x, mod, w, wg, conv_w, *, seq_len, conv_col0, b_key, name):
    g, r, k = x.shape
    n = w.shape[1]
    width, cch = conv_w.shape
    tm = _pick(r, (1024, 512, 256, 128, 64))
    tn = _pick(b_key, (1024, 512, 256, 128))
    assert n % tn == 0 and conv_col0 % tn == 0 and cch % tn == 0 and tm % HALO == 0 and r % seq_len == 0
    assert tm % seq_len == 0 or seq_len % tm == 0
    conv0, nconv = conv_col0 // tn, cch // tn
    hpt = tm // HALO
    nh = r // HALO
    gm = mod.shape[0]
    kern = functools.partial(_proj_ab_kernel, seq_len=seq_len, conv0=conv0, nconv=nconv,
                             qk_heads=2 * b_key // HEAD, q_heads=b_key // HEAD, width=width)
    return pl.pallas_call(
        kern,
        grid=(g, r // tm, n // tn),
        in_specs=[pl.BlockSpec((1, tm, k), lambda a, i, j: (a, i, 0)),
                  pl.BlockSpec((1, HALO, k), lambda a, i, j: (a, jnp.maximum(i * hpt - 1, 0), 0)),
                  pl.BlockSpec((1, HALO, k), lambda a, i, j: (a, jnp.minimum((i + 1) * hpt, nh - 1), 0)),
                  pl.BlockSpec((1, 6, k), lambda a, i, j: (a % gm, 0, 0)),
                  pl.BlockSpec((k, tn), lambda a, i, j: (0, j)),
                  pl.BlockSpec((k, HEAD), lambda a, i, j: (0, 0)),
                  pl.BlockSpec((width, tn), lambda a, i, j: (0, jnp.clip(j - conv0, 0, nconv - 1)))],
        out_specs=[pl.BlockSpec((1, tm, tn), lambda a, i, j: (a, i, j)),
                   pl.BlockSpec((1, tm, HEAD), lambda a, i, j: (a, i, 0))],
        out_shape=[jax.ShapeDtypeStruct((g, r, n), F32), jax.ShapeDtypeStruct((g, r, HEAD), F32)],
        scratch_shapes=[pltpu.VMEM((tm + 2 * HALO, k), BF16)],
        compiler_params=_cparams("parallel", "parallel", "arbitrary"),
        name=name,
    )(x, x, x, mod, w, wg, conv_w)


def _merge_half(of_ref, ob_ref, gate_ref, norm_ref, ys_ref, col0, nheads):
    for h in range(nheads):
        hs = slice(h * HEAD, (h + 1) * HEAD)
        o = of_ref[0, :, hs] + ob_ref[0, :, hs]
        ms = jnp.mean(o * o, axis=-1, keepdims=True)
        y = o * lax.rsqrt(ms + EPS) * norm_ref[...] * _silu(gate_ref[0, :, hs])
        ys_ref[:, col0 + h * HEAD:col0 + (h + 1) * HEAD] = y.astype(BF16)


def _outproj_kernel(*refs, merge, alpha, a_heads, b_heads):
    if merge:
        (oaf, oab, obf, obb, ga, zb, an, bn, w_ref, res_ref, mod_ref, g_ref, b_ref, o_ref, ys_ref) = refs
        _merge_half(oaf, oab, ga, an, ys_ref, 0, a_heads)
        _merge_half(obf, obb, zb, bn, ys_ref, a_heads * HEAD, b_heads)
        y = ys_ref[...]
    else:
        x_ref, w_ref, res_ref, mod_ref, g_ref, b_ref, o_ref = refs
        y = x_ref[0]
    acc = jnp.dot(y, w_ref[...], preferred_element_type=F32)
    r = alpha * res_ref[0] + mod_ref[0, 2:3, :] * acc
    o_ref[0] = _layernorm_rows(r, g_ref[...], b_ref[...])


def _outproj(xs, w, res, mod, ln_g, ln_b, *, alpha, merge=None, name):
    g, r, d = res.shape
    k = w.shape[0]
    tm = _pick(r, (512, 256, 128, 64) if merge is None else (256, 128, 64))
    gm = mod.shape[0]
    row = lambda a, i: (a, i, 0)
    fixed = lambda a, i: (0, 0)
    if merge is None:
        in_specs = [pl.BlockSpec((1, tm, k), row)]
        args = [xs]
        scratch = []
        kern = functools.partial(_outproj_kernel, merge=False, alpha=alpha, a_heads=0, b_heads=0)
    else:
        oaf, oab, obf, obb, p, an, bn = xs
        a_val, b_val, ga_blk, zb_blk = merge
        in_specs = [pl.BlockSpec((1, tm, a_val), row), pl.BlockSpec((1, tm, a_val), row),
                    pl.BlockSpec((1, tm, b_val), row), pl.BlockSpec((1, tm, b_val), row),
                    pl.BlockSpec((1, tm, a_val), lambda a, i: (a, i, ga_blk)),
                    pl.BlockSpec((1, tm, b_val), lambda a, i: (a, i, zb_blk)),
                    pl.BlockSpec((1, HEAD), fixed), pl.BlockSpec((1, HEAD), fixed)]
        args = [oaf, oab, obf, obb, p, p, an, bn]
        scratch = [pltpu.VMEM((tm, k), BF16)]
        kern = functools.partial(_outproj_kernel, merge=True, alpha=alpha,
                                 a_heads=a_val // HEAD, b_heads=b_val // HEAD)
    in_specs += [pl.BlockSpec((k, d), fixed, pipeline_mode=pl.Buffered(1)),
                 pl.BlockSpec((1, tm, d), row),
                 pl.BlockSpec((1, 6, d), lambda a, i: (a % gm, 0, 0)),
                 pl.BlockSpec((1, d), fixed), pl.BlockSpec((1, d), fixed)]
    args += [w, res, mod, ln_g.reshape(1, d), ln_b.reshape(1, d)]
    return pl.pallas_call(
        kern,
        grid=(g, r // tm),
        in_specs=in_specs,
        out_specs=pl.BlockSpec((1, tm, d), row),
        out_shape=jax.ShapeDtypeStruct((g, r, d), F32),
        scratch_shapes=scratch,
        compiler_params=_cparams("parallel", "parallel"),
        name=name,
    )(*args)


def _tri_masks(n):
    row = lax.broadcasted_iota(jnp.int32, (n, n), 0)
    col = lax.broadcasted_iota(jnp.int32, (n, n), 1)
    return col <= row, col >= row


def _gla_kernel(qf_ref, zf_ref, vf_ref, qb_ref, zb_ref, vb_ref, lb_ref, s0_ref,
                of_ref, ob_ref, sfin_ref, st_ref, *, hb, nchunk):
    t = pl.program_id(2)
    c = A_CHUNK

    @pl.when(t == 0)
    def _():
        st_ref[...] = s0_ref[0]

    causal, anti = _tri_masks(c)
    streams = ((qf_ref, zf_ref, vf_ref, of_ref, causal, c // 2 - 1, c - 1),
               (qb_ref, zb_ref, vb_ref, ob_ref, anti, c - c // 2, 0))

    def body(ci, carry):
        items = []
        for d, (q_ref, z_ref, v_ref, o_ref, mask, mid, last) in enumerate(streams):
            cc = ci if d == 0 else nchunk - 1 - ci
            sl = pl.ds(pl.multiple_of(cc * c, c), c)
            q = _silu(q_ref[0, sl, :])
            v = v_ref[0, sl, :]
            lb = lb_ref[d:d + 1, :]
            f = lb + (1.0 - lb) * _sigmoid(z_ref[0, sl, :])
            k = 1.0 - f
            cum = _tri_dot(mask.astype(BF16), jnp.log(f))
            m = cum[mid:mid + 1, :]
            bl = cum[last:last + 1, :]
            qm = q * jnp.exp(cum - m)
            km = k * jnp.exp(m - cum)
            qe = qm * jnp.exp(m)
            kd = km * jnp.exp(bl - m)
            dec = jnp.exp(bl)
            for h in range(hb):
                hs = slice(h * HEAD, (h + 1) * HEAD)
                items.append(dict(d=d, h=h, sl=sl, hs=hs, o_ref=o_ref, mask=mask, qe=qe[:, hs], qm=qm[:, hs],
                                  km=km[:, hs], kd=kd[:, hs], v=v[:, hs], dec=dec[:, hs]))
        for it in items:
            it["a"] = jnp.where(it["mask"], _dot_nt(it["qm"], it["km"]), 0.0)
            it["u"] = _dot_tn(it["v"], it["kd"])
        for it in items:
            it["st"] = st_ref[it["d"], it["h"]]
            it["o"] = _dot_nt(it["qe"], it["st"]) + _dot(it["a"], it["v"])
        for it in items:
            it["o_ref"][0, it["sl"], it["hs"]] = it["o"]
            st_ref[it["d"], it["h"]] = it["st"] * it["dec"] + it["u"]
        return carry

    lax.fori_loop(0, nchunk, body, 0)

    @pl.when(t == pl.num_programs(2) - 1)
    def _():
        sfin_ref[0] = st_ref[...]


def _gla(p, lb, s0, *, a_key, name):
    b, l, _ = p.shape
    heads = a_key // HEAD
    hb = _pick(heads, (8, 4, 2, 1))
    wb = hb * HEAD
    per = a_key // wb
    tb = _pick(l, (256, 128, 64, 32))
    nblk = l // tb
    fwd = lambda grp: (lambda bi, hg, t: (bi, t, grp * per + hg))
    bwd = lambda grp: (lambda bi, hg, t: (bi, nblk - 1 - t, grp * per + hg))
    blk = (1, tb, wb)
    st_blk = (1, 2, hb, HEAD, HEAD)
    st_map = lambda bi, hg, t: (bi, 0, hg, 0, 0)
    return pl.pallas_call(
        functools.partial(_gla_kernel, hb=hb, nchunk=tb // A_CHUNK),
        grid=(b, heads // hb, nblk),
        in_specs=[pl.BlockSpec(blk, fwd(0)), pl.BlockSpec(blk, fwd(1)), pl.BlockSpec(blk, fwd(3)),
                  pl.BlockSpec(blk, bwd(0)), pl.BlockSpec(blk, bwd(2)), pl.BlockSpec(blk, bwd(3)),
                  pl.BlockSpec((2, wb), lambda bi, hg, t: (0, hg)),
                  pl.BlockSpec(st_blk, st_map)],
        out_specs=[pl.BlockSpec(blk, lambda bi, hg, t: (bi, t, hg)),
                   pl.BlockSpec(blk, lambda bi, hg, t: (bi, nblk - 1 - t, hg)),
                   pl.BlockSpec(st_blk, st_map)],
        out_shape=[jax.ShapeDtypeStruct((b, l, a_key), F32),
                   jax.ShapeDtypeStruct((b, l, a_key), F32),
                   jax.ShapeDtypeStruct(s0.shape, F32)],
        scratch_shapes=[pltpu.VMEM((2, hb, HEAD, HEAD), F32)],
        compiler_params=_cparams("parallel", "parallel", "arbitrary"),
        name=name,
    )(p, p, p, p, p, p, lb, s0)


def _gdn_kernel(qf_ref, kf_ref, vf_ref, gcf_ref, grf_ref, qb_ref, kb_ref, vb_ref, gcb_ref, grb_ref,
                pc_ref, pr_ref, s0_ref, of_ref, ob_ref, sfin_ref, st_ref, *, hb, nchunk):
    t = pl.program_id(2)
    c = B_CHUNK

    @pl.when(t == 0)
    def _():
        st_ref[...] = s0_ref[0]

    row = lax.broadcasted_iota(jnp.int32, (c, HEAD), 0)
    col = lax.broadcasted_iota(jnp.int32, (c, HEAD), 1)
    right = col >= c
    eye_right = (col == row + c).astype(F32)
    causal, anti = _tri_masks(c)
    zeros_k = jnp.zeros((c, HEAD), F32)
    dirs = ((col <= row, col < row, causal.astype(BF16), ((col >= row) & ~right).astype(BF16), c - 1),
            ((col >= row) & ~right, (col > row) & ~right, anti.astype(BF16), (col <= row).astype(BF16), 0))
    streams = ((qf_ref, kf_ref, vf_ref, gcf_ref, grf_ref, of_ref), (qb_ref, kb_ref, vb_ref, gcb_ref, grb_ref, ob_ref))

    def body(ci, carry):
        items = []
        for d in range(2):
            q_ref, k_ref, v_ref, gc_ref, gr_ref, o_ref = streams[d]
            incl, strict, tri_c, tri_r, last = dirs[d]
            cc = ci if d == 0 else nchunk - 1 - ci
            sl = pl.ds(pl.multiple_of(cc * c, c), c)
            gcol = gc_ref[0, 0, sl, :]
            a_log = pr_ref[0, d:d + 1, 0:hb]
            dt = pr_ref[0, 2 + d:3 + d, 0:hb]
            g_col = _tri_dot(tri_c, -jnp.exp(a_log) * _softplus(gcol[:, d * hb:(d + 1) * hb] + dt))
            beta_col = _sigmoid(gcol[:, (2 + d) * hb:(3 + d) * hb])
            grow = gr_ref[0, 0, cc]
            a_log_c = pc_ref[0, :, d:d + 1]
            dt_c = pc_ref[0, :, 2 + d:3 + d]
            g_row = _dot_tri(-jnp.exp(a_log_c) * _softplus(grow[d * hb:(d + 1) * hb, :] + dt_c), tri_r)
            for hl in range(hb):
                hs = slice(hl * HEAD, (hl + 1) * HEAD)
                items.append(dict(d=d, hl=hl, sl=sl, hs=hs, o_ref=o_ref, incl=incl, strict=strict, last=last,
                                  gc=g_col[:, hl:hl + 1], bc=beta_col[:, hl:hl + 1], gr=g_row[hl:hl + 1, :],
                                  q=q_ref[0, sl, hs], k=k_ref[0, sl, hs], v=v_ref[0, sl, hs]))
        for it in items:
            it["decay"] = jnp.exp(jnp.where(it["incl"], it["gc"] - it["gr"], NEG))
            it["kb"] = it["k"] * it["bc"]
        for it in items:
            x = jnp.concatenate([it["kb"], it["q"]], axis=0)
            kpad = jnp.concatenate([it["k"], zeros_k], axis=0)
            it["kq"] = _dot_nt(x, kpad)
        for it in items:
            it["w"] = eye_right - jnp.where(it["strict"], it["kq"][:c] * it["decay"], 0.0)
            it["attn"] = it["kq"][c:] * it["decay"]
        for _ in range(int(np.log2(c))):
            rs = [_dot3(it["w"][:, :c], it["w"]) for it in items]
            for it, r in zip(items, rs):
                it["w"] = r + jnp.where(right, it["w"], 0.0)
        for it in items:
            tinv = pltpu.roll(it["w"], c, 1)[:, :c]
            egc = jnp.exp(it["gc"])
            rhs = jnp.concatenate([it["v"] * it["bc"], it["kb"] * egc], axis=1)
            it["sol"] = _dot3(tinv, rhs)
            g_last = it["gc"][it["last"]:it["last"] + 1, :]
            it["qe"] = it["q"] * egc
            it["kdt"] = jnp.transpose(it["k"] * jnp.exp(g_last - it["gc"]))
            it["dlast"] = jnp.exp(g_last)
        for it in items:
            it["s"] = st_ref[it["d"], it["hl"]]
            it["r"] = _dot(jnp.concatenate([it["sol"][:, HEAD:], it["qe"]], axis=0), it["s"])
        for it in items:
            it["vn"] = it["sol"][:, :HEAD] - it["r"][:c]
            it["r2"] = _dot(jnp.concatenate([it["attn"][:, :c], it["kdt"]], axis=0), it["vn"])
        for it in items:
            it["o_ref"][0, it["sl"], it["hs"]] = it["r"][c:] + it["r2"][:c]
            st_ref[it["d"], it["hl"]] = it["s"] * it["dlast"] + it["r2"][c:]
        return carry

    lax.fori_loop(0, nchunk, body, 0)

    @pl.when(t == pl.num_programs(2) - 1)
    def _():
        sfin_ref[0] = st_ref[...]


def _gdn(qkv, gates, prm, s0, *, b_key, grp0, name):
    b, l, _ = qkv.shape
    assert HEAD == 2 * B_CHUNK
    heads = b_key // HEAD
    hb = _pick(heads, (8, 4, 2, 1))
    nhg = heads // hb
    wb = hb * HEAD
    tb = _pick(l, (256, 128, 64))
    nblk = l // tb
    cpb = tb // B_CHUNK
    g4 = gates[:, :, :4 * heads].reshape(b, l, 4, nhg, hb)
    gcol = jnp.transpose(g4, (0, 3, 1, 2, 4)).reshape(b, nhg, l, 4 * hb)
    grow = jnp.swapaxes(gcol.reshape(b, nhg, l // B_CHUNK, B_CHUNK, 4 * hb), 3, 4)
    gcol = jnp.pad(gcol, ((0, 0), (0, 0), (0, 0), (0, HEAD - 4 * hb)))
    p4 = prm.reshape(4, nhg, hb)
    prow = jnp.pad(jnp.transpose(p4, (1, 0, 2)), ((0, 0), (0, 0), (0, HEAD - hb)))
    pcol = jnp.pad(jnp.transpose(p4, (1, 2, 0)), ((0, 0), (0, 0), (0, HEAD - 4)))
    fwd = lambda grp: (lambda bi, hg, t: (bi, t, (grp0 + grp) * nhg + hg))
    bwd = lambda grp: (lambda bi, hg, t: (bi, nblk - 1 - t, (grp0 + grp) * nhg + hg))
    blk = (1, tb, wb)
    gc_blk = (1, 1, tb, HEAD)
    gr_blk = (1, 1, cpb, 4 * hb, B_CHUNK)
    st_blk = (1, 2, hb, HEAD, HEAD)
    st_map = lambda bi, hg, t: (bi, 0, hg, 0, 0)
    return pl.pallas_call(
        functools.partial(_gdn_kernel, hb=hb, nchunk=cpb),
        grid=(b, nhg, nblk),
        in_specs=[pl.BlockSpec(blk, fwd(0)), pl.BlockSpec(blk, fwd(1)), pl.BlockSpec(blk, fwd(2)),
                  pl.BlockSpec(gc_blk, lambda bi, hg, t: (bi, hg, t, 0)),
                  pl.BlockSpec(gr_blk, lambda bi, hg, t: (bi, hg, t, 0, 0)),
                  pl.BlockSpec(blk, bwd(0)), pl.BlockSpec(blk, bwd(1)), pl.BlockSpec(blk, bwd(2)),
                  pl.BlockSpec(gc_blk, lambda bi, hg, t: (bi, hg, nblk - 1 - t, 0)),
                  pl.BlockSpec(gr_blk, lambda bi, hg, t: (bi, hg, nblk - 1 - t, 0, 0)),
                  pl.BlockSpec((1, hb, HEAD), lambda bi, hg, t: (hg, 0, 0)),
                  pl.BlockSpec((1, 4, HEAD), lambda bi, hg, t: (hg, 0, 0)),
                  pl.BlockSpec(st_blk, st_map)],
        out_specs=[pl.BlockSpec(blk, lambda bi, hg, t: (bi, t, hg)),
                   pl.BlockSpec(blk, lambda bi, hg, t: (bi, nblk - 1 - t, hg)),
                   pl.BlockSpec(st_blk, st_map)],
        out_shape=[jax.ShapeDtypeStruct((b, l, b_key), F32),
                   jax.ShapeDtypeStruct((b, l, b_key), F32),
                   jax.ShapeDtypeStruct(s0.shape, F32)],
        scratch_shapes=[pltpu.VMEM((2, hb, HEAD, HEAD), F32)],
        compiler_params=_cparams("parallel", "parallel", "arbitrary"),
        name=name,
    )(qkv, qkv, qkv, gcol, grow, qkv, qkv, qkv, gcol, grow, pcol, prow, s0)


def _rope(x, cos, s1, s2):
    q = HEAD // 4
    return x * cos + pltpu.roll(x, HEAD - q, 1) * s1 + pltpu.roll(x, q, 1) * s2


def _rope_tables(l):
    quarter = HEAD // 4
    pos = jnp.arange(l)
    inv = ROPE_BASE ** (-jnp.arange(quarter, dtype=F32) / quarter)
    zero = jnp.zeros((l, quarter), F32)
    cos, s1, s2 = [], [], []
    for p in ((pos // GRID_W).astype(F32), (pos % GRID_W).astype(F32)):
        ang = p[:, None] * inv[None, :]
        cos += [jnp.cos(ang), jnp.cos(ang)]
        s1 += [-jnp.sin(ang), zero]
        s2 += [zero, jnp.sin(ang)]
    return (jnp.concatenate(cos, -1), jnp.concatenate(s1, -1), jnp.concatenate(s2, -1))


def _nattn_plan(rows):
    wr = min(WIN_R, rows)
    tr = min(4, rows)
    kbr = min(rows, tr + wr)
    qc = np.arange(GRID_W)
    c0 = np.clip(qc - WIN_C // 2, 0, GRID_W - WIN_C)
    ok_c = (qc[None, :] >= c0[:, None]) & (qc[None, :] < c0[:, None] + WIN_C)
    dc = np.clip(qc[None, :] - qc[:, None] + WIN_C - 1, 0, 2 * WIN_C - 2)
    tiles, classes, keys = [], [], {}
    for t in range(rows // tr):
        ks = int(np.clip(tr * t - wr // 2, 0, rows - kbr))
        qr = tr * t + np.arange(tr)
        kr = ks + np.arange(kbr)
        r0 = np.clip(qr - wr // 2, 0, rows - wr)
        ok_r = (kr[None, :] >= r0[:, None]) & (kr[None, :] < r0[:, None] + wr)
        dr = np.clip(kr[None, :] - qr[:, None] + WIN_R - 1, 0, 2 * WIN_R - 2)
        valid = (ok_r[:, None, :, None] & ok_c[None, :, None, :]).reshape(tr * GRID_W, kbr * GRID_W)
        key = dr.tobytes() + valid.tobytes()
        if key not in keys:
            keys[key] = len(classes)
            classes.append((dr, valid))
        tiles.append((tr * t * GRID_W, ks * GRID_W, keys[key]))
    dr = np.stack([cl[0] for cl in classes]).astype(np.int32)
    valid = np.stack([cl[1] for cl in classes])
    return tuple(tiles), dr, dc.astype(np.int32), valid


def _nattn_bias(rel_bias, dr, dc, valid):
    h = rel_bias.shape[0]
    ncls, tr, kbr = dr.shape
    w = dc.shape[0]
    by_col = jnp.take(rel_bias, dc.reshape(-1), axis=2).reshape(h, -1, w, w)
    tab = jnp.take(by_col, dr.reshape(-1), axis=1).reshape(h, ncls, tr, kbr, w, w)
    tab = jnp.transpose(tab, (0, 1, 2, 4, 3, 5)).reshape(h, ncls, tr * w, kbr * w)
    return jnp.where(valid[None], tab, NEG)


def _nattn_kernel(q_ref, k_ref, v_ref, kc_ref, vc_ref, bias_ref, cos_ref, s1_ref, s2_ref,
                  o_ref, kr_ref, vr_ref, *, tiles):
    tq, tk = bias_ref.shape[2], bias_ref.shape[3]
    kr_ref[...] = _rope(k_ref[0], cos_ref[...], s1_ref[...], s2_ref[...]).astype(BF16)
    vr_ref[...] = v_ref[0].astype(BF16)
    kc = kc_ref[0].astype(BF16)
    vc = vc_ref[0].astype(BF16)
    for q0, k0, cls in tiles:
        qs = slice(q0, q0 + tq)
        qt = _rope(q_ref[0, qs, :], cos_ref[qs, :], s1_ref[qs, :], s2_ref[qs, :]) * HEAD ** -0.5
        qt = qt.astype(BF16)
        s_lat = _dot_nt(qt, kr_ref[k0:k0 + tk, :]) + bias_ref[0, cls]
        s_ctx = _dot_nt(qt, kc)
        m = jnp.maximum(jnp.max(s_lat, axis=-1, keepdims=True), jnp.max(s_ctx, axis=-1, keepdims=True))
        p_lat = jnp.exp(s_lat - m)
        p_ctx = jnp.exp(s_ctx - m)
        den = jnp.sum(p_lat, axis=-1, keepdims=True) + jnp.sum(p_ctx, axis=-1, keepdims=True)
        o = _dot(p_lat, vr_ref[k0:k0 + tk, :]) + _dot(p_ctx, vc)
        o_ref[0, qs, :] = (o / den).astype(o_ref.dtype)


def _nattn(p_lat, p_ctx, bias, tiles, ropes, *, heads, kc_blk, vc_blk, name):
    b, l, _ = p_lat.shape
    lc = p_ctx.shape[1]
    ncls, tq, tk = bias.shape[1:]
    tab = pl.BlockSpec((l, HEAD), lambda h, bi: (0, 0))
    return pl.pallas_call(
        functools.partial(_nattn_kernel, tiles=tiles),
        grid=(heads, b),
        in_specs=[pl.BlockSpec((1, l, HEAD), lambda h, bi: (bi, 0, h)),
                  pl.BlockSpec((1, l, HEAD), lambda h, bi: (bi, 0, heads + h)),
                  pl.BlockSpec((1, l, HEAD), lambda h, bi: (bi, 0, 2 * heads + h)),
                  pl.BlockSpec((1, lc, HEAD), lambda h, bi: (bi, 0, kc_blk + h)),
                  pl.BlockSpec((1, lc, HEAD), lambda h, bi: (bi, 0, vc_blk + h)),
                  pl.BlockSpec((1, ncls, tq, tk), lambda h, bi: (h, 0, 0, 0)),
                  tab, tab, tab],
        out_specs=pl.BlockSpec((1, l, HEAD), lambda h, bi: (bi, 0, h)),
        out_shape=jax.ShapeDtypeStruct((b, l, heads * HEAD), BF16),
        scratch_shapes=[pltpu.VMEM((l, HEAD), BF16), pltpu.VMEM((l, HEAD), BF16)],
        compiler_params=_cparams("parallel", "parallel"),
        name=name,
    )(p_lat, p_lat, p_lat, p_ctx, p_ctx, bias, *ropes)


def _cattn_kernel(q_ref, k_ref, v_ref, o_ref):
    s = _dot_nt(q_ref[0] * HEAD ** -0.5, k_ref[0])
    p = jnp.exp(s - jnp.max(s, axis=-1, keepdims=True))
    o = _dot(p, v_ref[0]) / jnp.sum(p, axis=-1, keepdims=True)
    o_ref[0] = o.astype(o_ref.dtype)


def _cattn(p_ctx, *, heads, name):
    b, lc, _ = p_ctx.shape
    blk = (1, lc, HEAD)
    return pl.pallas_call(
        _cattn_kernel,
        grid=(b, heads),
        in_specs=[pl.BlockSpec(blk, lambda bi, h: (bi, 0, h)),
                  pl.BlockSpec(blk, lambda bi, h: (bi, 0, heads + h)),
                  pl.BlockSpec(blk, lambda bi, h: (bi, 0, 2 * heads + h))],
        out_specs=pl.BlockSpec(blk, lambda bi, h: (bi, 0, h)),
        out_shape=jax.ShapeDtypeStruct((b, lc, heads * HEAD), BF16),
        compiler_params=_cparams("parallel", "parallel"),
        name=name,
    )(p_ctx, p_ctx, p_ctx)


def _ffn_up_kernel(x_ref, xp_ref, xn_ref, mod_ref, wa_ref, wg_ref, wdw_ref, bdw_ref, o_ref, xs_ref, *, seq_len):
    tm = x_ref.shape[1]
    i = pl.program_id(1)

    @pl.when(pl.program_id(2) == 0)
    def _():
        sc = 1.0 + mod_ref[0, 4:5, :]
        sh = mod_ref[0, 3:4, :]
        xs_ref[0:HALO, :] = (xp_ref[0] * sc + sh).astype(BF16)
        xs_ref[HALO:HALO + tm, :] = (x_ref[0] * sc + sh).astype(BF16)
        xs_ref[HALO + tm:, :] = (xn_ref[0] * sc + sh).astype(BF16)

    a = jnp.dot(xs_ref[...], wa_ref[...], preferred_element_type=F32)
    gate = jnp.dot(xs_ref[HALO:HALO + tm, :], wg_ref[...], preferred_element_type=F32)
    ext = tm + 2 * HALO
    a_prev = pltpu.roll(a, 1, 0)[HALO:HALO + tm]
    a_next = pltpu.roll(a, ext - 1, 0)[HALO:HALO + tm]
    pos = _seq_pos(i, tm, gate.shape, seq_len)
    a_prev = jnp.where(pos == 0, 0.0, a_prev)
    a_next = jnp.where(pos == seq_len - 1, 0.0, a_next)
    conv = (a_prev * wdw_ref[0:1, :] + a[HALO:HALO + tm] * wdw_ref[1:2, :] + a_next * wdw_ref[2:3, :]
            + bdw_ref[...])
    gelu = 0.5 * conv * (1.0 + lax.erf(conv * np.float32(np.sqrt(0.5))))
    o_ref[0] = (gelu * gate).astype(o_ref.dtype)


def _ffn_up(x, mod, wa, wg, wdw, bdw, *, seq_len, name):
    g, r, k = x.shape
    assert r % seq_len == 0
    f = wa.shape[1]
    tm = _pick(r, (1024, 512, 256, 128, 64, 32, 16))
    tn = _pick(f, (512, 256, 128))
    hpt = tm // HALO
    nh = r // HALO
    gm = mod.shape[0]
    return pl.pallas_call(
        functools.partial(_ffn_up_kernel, seq_len=seq_len),
        grid=(g, r // tm, f // tn),
        in_specs=[pl.BlockSpec((1, tm, k), lambda a, i, j: (a, i, 0)),
                  pl.BlockSpec((1, HALO, k), lambda a, i, j: (a, jnp.maximum(i * hpt - 1, 0), 0)),
                  pl.BlockSpec((1, HALO, k), lambda a, i, j: (a, jnp.minimum((i + 1) * hpt, nh - 1), 0)),
                  pl.BlockSpec((1, 6, k), lambda a, i, j: (a % gm, 0, 0)),
                  pl.BlockSpec((k, tn), lambda a, i, j: (0, j)),
                  pl.BlockSpec((k, tn), lambda a, i, j: (0, j)),
                  pl.BlockSpec((3, tn), lambda a, i, j: (0, j)),
                  pl.BlockSpec((1, tn), lambda a, i, j: (0, j))],
        out_specs=pl.BlockSpec((1, tm, tn), lambda a, i, j: (a, i, j)),
        out_shape=jax.ShapeDtypeStruct((g, r, f), BF16),
        scratch_shapes=[pltpu.VMEM((tm + 2 * HALO, k), BF16)],
        compiler_params=_cparams("parallel", "parallel", "arbitrary"),
        name=name,
    )(x, x, x, mod, wa, wg, wdw, bdw)


def _ffn_down_kernel(x_ref, w_ref, res_ref, mod_ref, g_ref, b_ref, o_ref, *, alpha):
    acc = jnp.dot(x_ref[0], w_ref[...], preferred_element_type=F32)
    r = alpha * res_ref[0] + mod_ref[0, 5:6, :] * acc
    o_ref[0] = _layernorm_rows(r, g_ref[...], b_ref[...])


def _ffn_down(x, w, res, mod, ln_g, ln_b, *, alpha, name):
    g, r, f = x.shape
    d = w.shape[1]
    tm = _pick(r, (256, 128, 64))
    gm = mod.shape[0]
    row = lambda a, i: (a, i, 0)
    fixed = lambda a, i: (0, 0)
    return pl.pallas_call(
        functools.partial(_ffn_down_kernel, alpha=alpha),
        grid=(g, r // tm),
        in_specs=[pl.BlockSpec((1, tm, f), row),
                  pl.BlockSpec((f, d), fixed, pipeline_mode=pl.Buffered(1)),
                  pl.BlockSpec((1, tm, d), row),
                  pl.BlockSpec((1, 6, d), lambda a, i: (a % gm, 0, 0)),
                  pl.BlockSpec((1, d), fixed), pl.BlockSpec((1, d), fixed)],
        out_specs=pl.BlockSpec((1, tm, d), row),
        out_shape=jax.ShapeDtypeStruct((g, r, d), F32),
        compiler_params=_cparams("parallel", "parallel"),
        name=name,
    )(x, w, res, mod, ln_g.reshape(1, d), ln_b.reshape(1, d))


def _pad_cols(w, n):
    return jnp.pad(w, ((0, 0), (0, n - w.shape[1])))


def _flat(a):
    return a.reshape(1, -1, a.shape[-1])


def kernel(x, c, ctx, c_ctx, w_ada, b_ada, ln_g, ln_b, w_in_ab, hgrn_lb, hgrn_norm, gdn_conv, gdn_a_log,
           gdn_dt_bias, gdn_norm, w_out_ab, w_in_c, na_rel_bias, w_out_c, ffn_w_up, ffn_w_dw, ffn_b_dw,
           ffn_w_down):
    b, l, d = x.shape
    lc = ctx.shape[1]
    depth = w_ada.shape[0]
    alpha = (2 * depth) ** 0.25
    a_key = hgrn_lb.shape[-1]
    b_heads = gdn_a_log.shape[-1]
    b_key = b_heads * HEAD
    b_val = gdn_conv.shape[-1] - 2 * b_key
    a_val = w_out_ab.shape[1] - b_val
    n_main = 3 * a_key + 2 * a_val + 2 * b_key + 2 * b_val
    c_heads = d // HEAD
    d_ff = ffn_w_down.shape[1]
    ffp = -(-d_ff // FF_ALIGN) * FF_ALIGN
    assert a_key == a_val == b_key == b_val and w_in_ab.shape[-1] == n_main + 4 * b_heads
    assert l % (4 * GRID_W) == 0 or l // GRID_W < 4

    rows = -(-(b + 1) // 8) * 8
    cc = jnp.concatenate([c, c_ctx[None, :], jnp.zeros((rows - b - 1, d), F32)], axis=0)
    mod = _ada(cc, w_ada, b_ada).reshape(depth, rows, 6, d)

    lb_all = jax.nn.softmax(hgrn_lb.astype(F32), axis=0)
    lb_all = jnp.cumsum(lb_all, axis=0) - lb_all[0]

    tiles, bias_dr, bias_dc, bias_ok = _nattn_plan(l // GRID_W)
    ropes = _rope_tables(l)
    s_zero = jnp.zeros((b, 2, a_key // HEAD, HEAD, HEAD), F32)

    xl, xc = x, ctx
    for layer in range(depth):
        need_ctx = layer < depth - 1
        mod_l = mod[layer, :b]
        mod_c = mod[layer, b:b + 1]
        g0, b0 = ln_g[layer, 0], ln_b[layer, 0]
        if layer % 2 == 0:
            e = layer // 2
            w_main = w_in_ab[e, :, :n_main].astype(BF16)
            w_gate = _pad_cols(w_in_ab[e, :, n_main:], HEAD).astype(BF16)
            w_out = w_out_ab[e].astype(BF16)
            an = hgrn_norm[e].reshape(1, HEAD)
            bn = gdn_norm[e].reshape(1, HEAD)
            gdn_prm = jnp.concatenate([gdn_a_log[e], gdn_dt_bias[e]], axis=0)
            merge = (a_val, b_val, (3 * a_key + a_val) // a_val, (n_main - b_val) // b_val)
            sa, sb = s_zero, s_zero
            for is_ctx, xs, mods in ((True, xc, mod_c), (False, xl, mod_l)):
                tag = f"l{layer}_{'ctx' if is_ctx else 'lat'}"
                ls = xs.shape[1]
                xin = _flat(xs) if is_ctx else xs
                p, gates = _proj_ab(xin, mods, w_main, w_gate, gdn_conv[e], seq_len=ls,
                                    conv_col0=3 * a_key + 2 * a_val, b_key=b_key, name=f"inproj_ab_{tag}")
                p, gates = p.reshape(b, ls, -1), gates.reshape(b, ls, -1)
                oaf, oab, sa = _gla(p, lb_all[e], sa, a_key=a_key, name=f"hgrn2_{tag}")
                obf, obb, sb = _gdn(p, gates, gdn_prm, sb, b_key=b_key,
                                    grp0=(3 * a_key + 2 * a_val) // b_key, name=f"gdn_{tag}")
                if is_ctx and not need_ctx:
                    continue
                y = _outproj((oaf, oab, obf, obb, p, an, bn), w_out, xs, mods,
                             g0, b0, alpha=alpha, merge=merge, name=f"outproj_ab_{tag}")
                if is_ctx:
                    xc = y
                else:
                    xl = y
        else:
            o = layer // 2
            w_in = w_in_c[o].astype(BF16)
            w_out = w_out_c[o].astype(BF16)
            bias = _nattn_bias(na_rel_bias[o], bias_dr, bias_dc, bias_ok)
            p_lat = _proj(xl, mod_l, w_in, shift=0, scale=1, name=f"inproj_c_l{layer}_lat")
            if need_ctx:
                p_ctx = _proj(_flat(xc), mod_c, w_in, shift=0, scale=1, name=f"inproj_c_l{layer}_ctx")
                kc_blk, vc_blk = c_heads, 2 * c_heads
            else:
                p_ctx = _proj(_flat(xc), mod_c, w_in[:, d:], shift=0, scale=1, name=f"inproj_c_l{layer}_ctx")
                kc_blk, vc_blk = 0, c_heads
            p_ctx = p_ctx.reshape(b, lc, -1)
            o_lat = _nattn(p_lat, p_ctx, bias, tiles, ropes, heads=c_heads, kc_blk=kc_blk, vc_blk=vc_blk,
                           name=f"nattn_l{layer}")
            xl = _outproj(o_lat, w_out, xl, mod_l, g0, b0, alpha=alpha, name=f"outproj_c_l{layer}_lat")
            if need_ctx:
                o_ctx = _cattn(p_ctx, heads=c_heads, name=f"cattn_l{layer}")
                xc = _outproj(o_ctx, w_out, xc, mod_c, g0, b0, alpha=alpha, name=f"outproj_c_l{layer}_ctx")

        wa = _pad_cols(ffn_w_up[layer, :, :d_ff], ffp).astype(BF16)
        wg = _pad_cols(ffn_w_up[layer, :, d_ff:], ffp).astype(BF16)
        wdw = _pad_cols(ffn_w_dw[layer], ffp)
        bdw = _pad_cols(ffn_b_dw[layer].reshape(1, d_ff), ffp)
        wdn = jnp.pad(ffn_w_down[layer], ((0, ffp - d_ff), (0, 0))).astype(BF16)
        g1, b1 = ln_g[layer, 1], ln_b[layer, 1]
        for is_ctx, xs, mods in ((True, xc, mod_c), (False, xl, mod_l)):
            if is_ctx and not need_ctx:
                continue
            tag = f"l{layer}_{'ctx' if is_ctx else 'lat'}"
            ls = xs.shape[1]
            hmid = _ffn_up(_flat(xs) if is_ctx else xs, mods, wa, wg, wdw, bdw, seq_len=ls,
                           name=f"ffn_up_{tag}").reshape(b, ls, ffp)
            y = _ffn_down(hmid, wdn, xs, mods, g1, b1, alpha=alpha, name=f"ffn_down_{tag}")
            if is_ctx:
                xc = y
            else:
                xl = y
    return xl
```

```python
import functools

import numpy as np
import jax
import jax.numpy as jnp
from jax import lax
from jax.experimental import pallas as pl
from jax.experimental.pallas import tpu as pltpu

F32 = jnp.float32
BF16 = jnp.bfloat16

HEAD = 128
GRID_W = 64
WIN_R = 8
WIN_C = 16
ROPE_BASE = 10000.0
A_CHUNK = 32
B_CHUNK = 64
EPS = 1e-6
NEG = -1e30
HALO = 16
FF_ALIGN = 512
VMEM_LIMIT = 56 * 1024 * 1024


def _cparams(*sem):
    return pltpu.CompilerParams(dimension_semantics=sem, vmem_limit_bytes=VMEM_LIMIT)


def _pick(n, prefs):
    for p in prefs:
        if n % p == 0:
            return p
    return n


def _sigmoid(x):
    return 1.0 / (1.0 + jnp.exp(-x))


def _silu(x):
    return x * _sigmoid(x)


def _softplus(x):
    return jnp.maximum(x, 0.0) + jnp.log(1.0 + jnp.exp(-jnp.abs(x)))


def _dot(a, b):
    return jnp.dot(a.astype(BF16), b.astype(BF16), preferred_element_type=F32)


def _dot_nt(a, b):
    return lax.dot_general(a.astype(BF16), b.astype(BF16), (((1,), (1,)), ((), ())),
                           preferred_element_type=F32)


def _dot_tn(a, b):
    return lax.dot_general(a.astype(BF16), b.astype(BF16), (((0,), (0,)), ((), ())),
                           preferred_element_type=F32)


def _split2(x):
    hi = x.astype(BF16)
    return hi, (x - hi.astype(F32)).astype(BF16)


def _mm(a, b):
    return jnp.dot(a, b, preferred_element_type=F32)


def _dot3(a, b):
    ah, al = _split2(a)
    bh, bl = _split2(b)
    return _mm(ah, bh) + _mm(ah, bl) + _mm(al, bh)


def _split3(x):
    x1 = x.astype(BF16)
    r = x - x1.astype(F32)
    x2 = r.astype(BF16)
    return x1, x2, (r - x2.astype(F32)).astype(BF16)


def _tri_dot(tri, x):
    return _mm(jnp.concatenate([tri, tri, tri], axis=1), jnp.concatenate(_split3(x), axis=0))


def _dot_tri(x, tri):
    return _mm(jnp.concatenate(_split3(x), axis=1), jnp.concatenate([tri, tri, tri], axis=0))


def _layernorm_rows(r, g, b):
    mu = jnp.mean(r, axis=-1, keepdims=True)
    rc = r - mu
    var = jnp.mean(rc * rc, axis=-1, keepdims=True)
    return rc * lax.rsqrt(var + EPS) * g + b


def _ada_kernel(c_ref, w_ref, b_ref, o_ref):
    s = _silu(c_ref[...])
    o_ref[0] = _dot(s, w_ref[0]) + b_ref[0]


def _ada(cc, w_ada, b_ada):
    depth, d, n = w_ada.shape
    rows = cc.shape[0]
    tn = _pick(n, (1024, 512, 256, 128))
    return pl.pallas_call(
        _ada_kernel,
        grid=(depth, n // tn),
        in_specs=[pl.BlockSpec((rows, d), lambda l, j: (0, 0)),
                  pl.BlockSpec((1, d, tn), lambda l, j: (l, 0, j)),
                  pl.BlockSpec((1, 1, tn), lambda l, j: (l, 0, j))],
        out_specs=pl.BlockSpec((1, rows, tn), lambda l, j: (l, 0, j)),
        out_shape=jax.ShapeDtypeStruct((depth, rows, n), F32),
        compiler_params=_cparams("parallel", "parallel"),
        name="ada_modulation",
    )(cc, w_ada, b_ada.reshape(depth, 1, n))


def _proj_kernel(x_ref, mod_ref, w_ref, o_ref, xs_ref, *, shift, scale):
    @pl.when(pl.program_id(2) == 0)
    def _():
        h = x_ref[0] * (1.0 + mod_ref[0, scale:scale + 1, :]) + mod_ref[0, shift:shift + 1, :]
        xs_ref[...] = h.astype(BF16)

    o_ref[0] = jnp.dot(xs_ref[...], w_ref[...], preferred_element_type=F32).astype(o_ref.dtype)


def _proj(x, mod, w, *, shift, scale, out_dtype=F32, name):
    g, r, k = x.shape
    n = w.shape[1]
    tm = _pick(r, (1024, 512, 256, 128, 64))
    tn = _pick(n, (1024, 512, 256, 128))
    gm = mod.shape[0]
    return pl.pallas_call(
        functools.partial(_proj_kernel, shift=shift, scale=scale),
        grid=(g, r // tm, n // tn),
        in_specs=[pl.BlockSpec((1, tm, k), lambda a, i, j: (a, i, 0)),
                  pl.BlockSpec((1, 6, k), lambda a, i, j: (a % gm, 0, 0)),
                  pl.BlockSpec((k, tn), lambda a, i, j: (0, j))],
        out_specs=pl.BlockSpec((1, tm, tn), lambda a, i, j: (a, i, j)),
        out_shape=jax.ShapeDtypeStruct((g, r, n), out_dtype),
        scratch_shapes=[pltpu.VMEM((tm, k), BF16)],
        compiler_params=_cparams("parallel", "parallel", "arbitrary"),
        name=name,
    )(x, mod, w)


def _seq_pos(i, tm, shape, seq_len):
    return (i * tm + lax.broadcasted_iota(jnp.int32, shape, 0)) % seq_len


def _proj_ab_kernel(x_ref, xp_ref, xn_ref, mod_ref, w_ref, wg_ref, cw_ref, o_ref, og_ref, xs_ref, *,
                    seq_len, conv0, nconv, qk_heads, q_heads, width):
    tm, tn = o_ref.shape[1], o_ref.shape[2]
    i = pl.program_id(1)
    j = pl.program_id(2)

    @pl.when(j == 0)
    def _():
        sc = 1.0 + mod_ref[0, 1:2, :]
        sh = mod_ref[0, 0:1, :]
        xs_ref[0:HALO, :] = (xp_ref[0] * sc + sh).astype(BF16)
        xs_ref[HALO:HALO + tm, :] = (x_ref[0] * sc + sh).astype(BF16)
        xs_ref[HALO + tm:, :] = (xn_ref[0] * sc + sh).astype(BF16)
        og_ref[0] = jnp.dot(xs_ref[HALO:HALO + tm, :], wg_ref[...], preferred_element_type=F32)

    is_conv = (j >= conv0) & (j < conv0 + nconv)

    @pl.when(jnp.logical_not(is_conv))
    def _():
        o_ref[0] = jnp.dot(xs_ref[HALO:HALO + tm, :], w_ref[...], preferred_element_type=F32)

    @pl.when(is_conv)
    def _():
        a = jnp.dot(xs_ref[...], w_ref[...], preferred_element_type=F32)
        ext = tm + 2 * HALO
        pos = _seq_pos(i, tm, (tm, tn), seq_len)
        pad = width // 2
        acc = a[HALO:HALO + tm] * cw_ref[pad:pad + 1, :]
        for t in range(width):
            s = t - pad
            if s == 0:
                continue
            shifted = pltpu.roll(a, (-s) % ext, 0)[HALO:HALO + tm]
            ok = (pos + s >= 0) & (pos + s < seq_len)
            acc = acc + jnp.where(ok, shifted, 0.0) * cw_ref[t:t + 1, :]
        y = _silu(acc)
        hpt = tn // HEAD
        for h in range(hpt):
            hs = slice(h * HEAD, (h + 1) * HEAD)
            head = (j - conv0) * hpt + h
            yh = y[:, hs]
            inv = lax.rsqrt(jnp.sum(yh * yh, axis=-1, keepdims=True) + EPS)
            gain = jnp.where(head < qk_heads, inv, 1.0) * jnp.where(head < q_heads, HEAD ** -0.5, 1.0)
            o_ref[0, :, hs] = yh * gain


def _proj_ab(x, mod, w, wg, conv_w, *, seq_len, conv_col0, b_key, name):
    g, r, k = x.shape
    n = w.shape[1]
    width, cch = conv_w.shape
    tm = _pick(r, (1024, 512, 256, 128, 64))
    tn = _pick(b_key, (1024, 512, 256, 128))
    assert n % tn == 0 and conv_col0 % tn == 0 and cch % tn == 0 and tm % HALO == 0 and r % seq_len == 0
    assert tm % seq_len == 0 or seq_len % tm == 0
    conv0, nconv = conv_col0 // tn, cch // tn
    hpt = tm // HALO
    nh = r // HALO
    gm = mod.shape[0]
    kern = functools.partial(_proj_ab_kernel, seq_len=seq_len, conv0=conv0, nconv=nconv,
                             qk_heads=2 * b_key // HEAD, q_heads=b_key // HEAD, width=width)
    return pl.pallas_call(
        kern,
        grid=(g, r // tm, n // tn),
        in_specs=[pl.BlockSpec((1, tm, k), lambda a, i, j: (a, i, 0)),
                  pl.BlockSpec((1, HALO, k), lambda a, i, j: (a, jnp.maximum(i * hpt - 1, 0), 0)),
                  pl.BlockSpec((1, HALO, k), lambda a, i, j: (a, jnp.minimum((i + 1) * hpt, nh - 1), 0)),
                  pl.BlockSpec((1, 6, k), lambda a, i, j: (a % gm, 0, 0)),
                  pl.BlockSpec((k, tn), lambda a, i, j: (0, j)),
                  pl.BlockSpec((k, HEAD), lambda a, i, j: (0, 0)),
                  pl.BlockSpec((width, tn), lambda a, i, j: (0, jnp.clip(j - conv0, 0, nconv - 1)))],
        out_specs=[pl.BlockSpec((1, tm, tn), lambda a, i, j: (a, i, j)),
                   pl.BlockSpec((1, tm, HEAD), lambda a, i, j: (a, i, 0))],
        out_shape=[jax.ShapeDtypeStruct((g, r, n), F32), jax.ShapeDtypeStruct((g, r, HEAD), F32)],
        scratch_shapes=[pltpu.VMEM((tm + 2 * HALO, k), BF16)],
        compiler_params=_cparams("parallel", "parallel", "arbitrary"),
        name=name,
    )(x, x, x, mod, w, wg, conv_w)


def _merge_half(of_ref, ob_ref, gate_ref, norm_ref, ys_ref, col0, nheads):
    for h in range(nheads):
        hs = slice(h * HEAD, (h + 1) * HEAD)
        o = of_ref[0, :, hs] + ob_ref[0, :, hs]
        ms = jnp.mean(o * o, axis=-1, keepdims=True)
        y = o * lax.rsqrt(ms + EPS) * norm_ref[...] * _silu(gate_ref[0, :, hs])
        ys_ref[:, col0 + h * HEAD:col0 + (h + 1) * HEAD] = y.astype(BF16)


def _outproj_kernel(*refs, merge, alpha, a_heads, b_heads):
    if merge:
        (oaf, oab, obf, obb, ga, zb, an, bn, w_ref, res_ref, mod_ref, g_ref, b_ref, o_ref, ys_ref) = refs
        _merge_half(oaf, oab, ga, an, ys_ref, 0, a_heads)
        _merge_half(obf, obb, zb, bn, ys_ref, a_heads * HEAD, b_heads)
        y = ys_ref[...]
    else:
        x_ref, w_ref, res_ref, mod_ref, g_ref, b_ref, o_ref = refs
        y = x_ref[0]
    acc = jnp.dot(y, w_ref[...], preferred_element_type=F32)
    r = alpha * res_ref[0] + mod_ref[0, 2:3, :] * acc
    o_ref[0] = _layernorm_rows(r, g_ref[...], b_ref[...])


def _outproj(xs, w, res, mod, ln_g, ln_b, *, alpha, merge=None, name):
    g, r, d = res.shape
    k = w.shape[0]
    tm = _pick(r, (512, 256, 128, 64) if merge is None else (256, 128, 64))
    gm = mod.shape[0]
    row = lambda a, i: (a, i, 0)
    fixed = lambda a, i: (0, 0)
    if merge is None:
        in_specs = [pl.BlockSpec((1, tm, k), row)]
        args = [xs]
        scratch = []
        kern = functools.partial(_outproj_kernel, merge=False, alpha=alpha, a_heads=0, b_heads=0)
    else:
        oaf, oab, obf, obb, p, an, bn = xs
        a_val, b_val, ga_blk, zb_blk = merge
        in_specs = [pl.BlockSpec((1, tm, a_val), row), pl.BlockSpec((1, tm, a_val), row),
                    pl.BlockSpec((1, tm, b_val), row), pl.BlockSpec((1, tm, b_val), row),
                    pl.BlockSpec((1, tm, a_val), lambda a, i: (a, i, ga_blk)),
                    pl.BlockSpec((1, tm, b_val), lambda a, i: (a, i, zb_blk)),
                    pl.BlockSpec((1, HEAD), fixed), pl.BlockSpec((1, HEAD), fixed)]
        args = [oaf, oab, obf, obb, p, p, an, bn]
        scratch = [pltpu.VMEM((tm, k), BF16)]
        kern = functools.partial(_outproj_kernel, merge=True, alpha=alpha,
                                 a_heads=a_val // HEAD, b_heads=b_val // HEAD)
    in_specs += [pl.BlockSpec((k, d), fixed, pipeline_mode=pl.Buffered(1)),
                 pl.BlockSpec((1, tm, d), row),
                 pl.BlockSpec((1, 6, d), lambda a, i: (a % gm, 0, 0)),
                 pl.BlockSpec((1, d), fixed), pl.BlockSpec((1, d), fixed)]
    args += [w, res, mod, ln_g.reshape(1, d), ln_b.reshape(1, d)]
    return pl.pallas_call(
        kern,
        grid=(g, r // tm),
        in_specs=in_specs,
        out_specs=pl.BlockSpec((1, tm, d), row),
        out_shape=jax.ShapeDtypeStruct((g, r, d), F32),
        scratch_shapes=scratch,
        compiler_params=_cparams("parallel", "parallel"),
        name=name,
    )(*args)


def _tri_masks(n):
    row = lax.broadcasted_iota(jnp.int32, (n, n), 0)
    col = lax.broadcasted_iota(jnp.int32, (n, n), 1)
    return col <= row, col >= row


def _gla_kernel(qf_ref, zf_ref, vf_ref, qb_ref, zb_ref, vb_ref, lb_ref, s0_ref,
                of_ref, ob_ref, sfin_ref, st_ref, *, hb, nchunk):
    t = pl.program_id(2)
    c = A_CHUNK

    @pl.when(t == 0)
    def _():
        st_ref[...] = s0_ref[0]

    causal, anti = _tri_masks(c)
    streams = ((qf_ref, zf_ref, vf_ref, of_ref, causal, c // 2 - 1, c - 1),
               (qb_ref, zb_ref, vb_ref, ob_ref, anti, c - c // 2, 0))

    def body(ci, carry):
        items = []
        for d, (q_ref, z_ref, v_ref, o_ref, mask, mid, last) in enumerate(streams):
            cc = ci if d == 0 else nchunk - 1 - ci
            sl = pl.ds(pl.multiple_of(cc * c, c), c)
            q = _silu(q_ref[0, sl, :])
            v = v_ref[0, sl, :]
            lb = lb_ref[d:d + 1, :]
            f = lb + (1.0 - lb) * _sigmoid(z_ref[0, sl, :])
            k = 1.0 - f
            cum = _tri_dot(mask.astype(BF16), jnp.log(f))
            m = cum[mid:mid + 1, :]
            bl = cum[last:last + 1, :]
            qm = q * jnp.exp(cum - m)
            km = k * jnp.exp(m - cum)
            qe = qm * jnp.exp(m)
            kd = km * jnp.exp(bl - m)
            dec = jnp.exp(bl)
            for h in range(hb):
                hs = slice(h * HEAD, (h + 1) * HEAD)
                items.append(dict(d=d, h=h, sl=sl, hs=hs, o_ref=o_ref, mask=mask, qe=qe[:, hs], qm=qm[:, hs],
                                  km=km[:, hs], kd=kd[:, hs], v=v[:, hs], dec=dec[:, hs]))
        for it in items:
            it["a"] = jnp.where(it["mask"], _dot_nt(it["qm"], it["km"]), 0.0)
            it["u"] = _dot_tn(it["v"], it["kd"])
        for it in items:
            it["st"] = st_ref[it["d"], it["h"]]
            it["o"] = _dot_nt(it["qe"], it["st"]) + _dot(it["a"], it["v"])
        for it in items:
            it["o_ref"][0, it["sl"], it["hs"]] = it["o"]
            st_ref[it["d"], it["h"]] = it["st"] * it["dec"] + it["u"]
        return carry

    lax.fori_loop(0, nchunk, body, 0, unroll=2)

    @pl.when(t == pl.num_programs(2) - 1)
    def _():
        sfin_ref[0] = st_ref[...]


def _gla(p, lb, s0, *, a_key, name):
    b, l, _ = p.shape
    heads = a_key // HEAD
    hb = _pick(heads, (8, 4, 2, 1))
    wb = hb * HEAD
    per = a_key // wb
    tb = _pick(l, (256, 128, 64, 32))
    nblk = l // tb
    fwd = lambda grp: (lambda bi, hg, t: (bi, t, grp * per + hg))
    bwd = lambda grp: (lambda bi, hg, t: (bi, nblk - 1 - t, grp * per + hg))
    blk = (1, tb, wb)
    st_blk = (1, 2, hb, HEAD, HEAD)
    st_map = lambda bi, hg, t: (bi, 0, hg, 0, 0)
    return pl.pallas_call(
        functools.partial(_gla_kernel, hb=hb, nchunk=tb // A_CHUNK),
        grid=(b, heads // hb, nblk),
        in_specs=[pl.BlockSpec(blk, fwd(0)), pl.BlockSpec(blk, fwd(1)), pl.BlockSpec(blk, fwd(3)),
                  pl.BlockSpec(blk, bwd(0)), pl.BlockSpec(blk, bwd(2)), pl.BlockSpec(blk, bwd(3)),
                  pl.BlockSpec((2, wb), lambda bi, hg, t: (0, hg)),
                  pl.BlockSpec(st_blk, st_map)],
        out_specs=[pl.BlockSpec(blk, lambda bi, hg, t: (bi, t, hg)),
                   pl.BlockSpec(blk, lambda bi, hg, t: (bi, nblk - 1 - t, hg)),
                   pl.BlockSpec(st_blk, st_map)],
        out_shape=[jax.ShapeDtypeStruct((b, l, a_key), F32),
                   jax.ShapeDtypeStruct((b, l, a_key), F32),
                   jax.ShapeDtypeStruct(s0.shape, F32)],
        scratch_shapes=[pltpu.VMEM((2, hb, HEAD, HEAD), F32)],
        compiler_params=_cparams("parallel", "parallel", "arbitrary"),
        name=name,
    )(p, p, p, p, p, p, lb, s0)


def _gdn_kernel(qf_ref, kf_ref, vf_ref, gcf_ref, grf_ref, qb_ref, kb_ref, vb_ref, gcb_ref, grb_ref,
                pc_ref, pr_ref, s0_ref, of_ref, ob_ref, sfin_ref, st_ref, *, hb, nchunk):
    t = pl.program_id(2)
    c = B_CHUNK

    @pl.when(t == 0)
    def _():
        st_ref[...] = s0_ref[0]

    row = lax.broadcasted_iota(jnp.int32, (c, HEAD), 0)
    col = lax.broadcasted_iota(jnp.int32, (c, HEAD), 1)
    right = col >= c
    eye_right = (col == row + c).astype(F32)
    causal, anti = _tri_masks(c)
    zeros_k = jnp.zeros((c, HEAD), F32)
    dirs = ((col <= row, col < row, causal.astype(BF16), ((col >= row) & ~right).astype(BF16), c - 1),
            ((col >= row) & ~right, (col > row) & ~right, anti.astype(BF16), (col <= row).astype(BF16), 0))
    streams = ((qf_ref, kf_ref, vf_ref, gcf_ref, grf_ref, of_ref), (qb_ref, kb_ref, vb_ref, gcb_ref, grb_ref, ob_ref))

    def body(ci, carry):
        items = []
        for d in range(2):
            q_ref, k_ref, v_ref, gc_ref, gr_ref, o_ref = streams[d]
            incl, strict, tri_c, tri_r, last = dirs[d]
            cc = ci if d == 0 else nchunk - 1 - ci
            sl = pl.ds(pl.multiple_of(cc * c, c), c)
            gcol = gc_ref[0, 0, sl, :]
            a_log = pr_ref[0, d:d + 1, 0:hb]
            dt = pr_ref[0, 2 + d:3 + d, 0:hb]
            g_col = _tri_dot(tri_c, -jnp.exp(a_log) * _softplus(gcol[:, d * hb:(d + 1) * hb] + dt))
            beta_col = _sigmoid(gcol[:, (2 + d) * hb:(3 + d) * hb])
            grow = gr_ref[0, 0, cc]
            a_log_c = pc_ref[0, :, d:d + 1]
            dt_c = pc_ref[0, :, 2 + d:3 + d]
            g_row = _dot_tri(-jnp.exp(a_log_c) * _softplus(grow[d * hb:(d + 1) * hb, :] + dt_c), tri_r)
            for hl in range(hb):
                hs = slice(hl * HEAD, (hl + 1) * HEAD)
                items.append(dict(d=d, hl=hl, sl=sl, hs=hs, o_ref=o_ref, incl=incl, strict=strict, last=last,
                                  gc=g_col[:, hl:hl + 1], bc=beta_col[:, hl:hl + 1], gr=g_row[hl:hl + 1, :],
                                  q=q_ref[0, sl, hs], k=k_ref[0, sl, hs], v=v_ref[0, sl, hs]))
        for it in items:
            it["decay"] = jnp.exp(jnp.where(it["incl"], it["gc"] - it["gr"], NEG))
            it["kb"] = it["k"] * it["bc"]
        for it in items:
            x = jnp.concatenate([it["kb"], it["q"]], axis=0)
            kpad = jnp.concatenate([it["k"], zeros_k], axis=0)
            it["kq"] = _dot_nt(x, kpad)
        for it in items:
            it["w"] = eye_right - jnp.where(it["strict"], it["kq"][:c] * it["decay"], 0.0)
            it["attn"] = it["kq"][c:] * it["decay"]
        for _ in range(int(np.log2(c))):
            rs = [_dot3(it["w"][:, :c], it["w"]) for it in items]
            for it, r in zip(items, rs):
                it["w"] = r + jnp.where(right, it["w"], 0.0)
        for it in items:
            tinv = pltpu.roll(it["w"], c, 1)[:, :c]
            egc = jnp.exp(it["gc"])
            rhs = jnp.concatenate([it["v"] * it["bc"], it["kb"] * egc], axis=1)
            it["sol"] = _dot3(tinv, rhs)
            g_last = it["gc"][it["last"]:it["last"] + 1, :]
            it["qe"] = it["q"] * egc
            it["kdt"] = jnp.transpose(it["k"] * jnp.exp(g_last - it["gc"]))
            it["dlast"] = jnp.exp(g_last)
        for it in items:
            it["s"] = st_ref[it["d"], it["hl"]]
            it["r"] = _dot(jnp.concatenate([it["sol"][:, HEAD:], it["qe"]], axis=0), it["s"])
        for it in items:
            it["vn"] = it["sol"][:, :HEAD] - it["r"][:c]
            it["r2"] = _dot(jnp.concatenate([it["attn"][:, :c], it["kdt"]], axis=0), it["vn"])
        for it in items:
            it["o_ref"][0, it["sl"], it["hs"]] = it["r"][c:] + it["r2"][:c]
            st_ref[it["d"], it["hl"]] = it["s"] * it["dlast"] + it["r2"][c:]
        return carry

    lax.fori_loop(0, nchunk, body, 0)

    @pl.when(t == pl.num_programs(2) - 1)
    def _():
        sfin_ref[0] = st_ref[...]


def _gdn(qkv, gates, prm, s0, *, b_key, grp0, name):
    b, l, _ = qkv.shape
    assert HEAD == 2 * B_CHUNK
    heads = b_key // HEAD
    hb = _pick(heads, (8, 4, 2, 1))
    nhg = heads // hb
    wb = hb * HEAD
    tb = _pick(l, (256, 128, 64))
    nblk = l // tb
    cpb = tb // B_CHUNK
    g4 = gates[:, :, :4 * heads].reshape(b, l, 4, nhg, hb)
    gcol = jnp.transpose(g4, (0, 3, 1, 2, 4)).reshape(b, nhg, l, 4 * hb)
    grow = jnp.swapaxes(gcol.reshape(b, nhg, l // B_CHUNK, B_CHUNK, 4 * hb), 3, 4)
    gcol = jnp.pad(gcol, ((0, 0), (0, 0), (0, 0), (0, HEAD - 4 * hb)))
    p4 = prm.reshape(4, nhg, hb)
    prow = jnp.pad(jnp.transpose(p4, (1, 0, 2)), ((0, 0), (0, 0), (0, HEAD - hb)))
    pcol = jnp.pad(jnp.transpose(p4, (1, 2, 0)), ((0, 0), (0, 0), (0, HEAD - 4)))
    fwd = lambda grp: (lambda bi, hg, t: (bi, t, (grp0 + grp) * nhg + hg))
    bwd = lambda grp: (lambda bi, hg, t: (bi, nblk - 1 - t, (grp0 + grp) * nhg + hg))
    blk = (1, tb, wb)
    gc_blk = (1, 1, tb, HEAD)
    gr_blk = (1, 1, cpb, 4 * hb, B_CHUNK)
    st_blk = (1, 2, hb, HEAD, HEAD)
    st_map = lambda bi, hg, t: (bi, 0, hg, 0, 0)
    return pl.pallas_call(
        functools.partial(_gdn_kernel, hb=hb, nchunk=cpb),
        grid=(b, nhg, nblk),
        in_specs=[pl.BlockSpec(blk, fwd(0)), pl.BlockSpec(blk, fwd(1)), pl.BlockSpec(blk, fwd(2)),
                  pl.BlockSpec(gc_blk, lambda bi, hg, t: (bi, hg, t, 0)),
                  pl.BlockSpec(gr_blk, lambda bi, hg, t: (bi, hg, t, 0, 0)),
                  pl.BlockSpec(blk, bwd(0)), pl.BlockSpec(blk, bwd(1)), pl.BlockSpec(blk, bwd(2)),
                  pl.BlockSpec(gc_blk, lambda bi, hg, t: (bi, hg, nblk - 1 - t, 0)),
                  pl.BlockSpec(gr_blk, lambda bi, hg, t: (bi, hg, nblk - 1 - t, 0, 0)),
                  pl.BlockSpec((1, hb, HEAD), lambda bi, hg, t: (hg, 0, 0)),
                  pl.BlockSpec((1, 4, HEAD), lambda bi, hg, t: (hg, 0, 0)),
                  pl.BlockSpec(st_blk, st_map)],
        out_specs=[pl.BlockSpec(blk, lambda bi, hg, t: (bi, t, hg)),
                   pl.BlockSpec(blk, lambda bi, hg, t: (bi, nblk - 1 - t, hg)),
                   pl.BlockSpec(st_blk, st_map)],
        out_shape=[jax.ShapeDtypeStruct((b, l, b_key), F32),
                   jax.ShapeDtypeStruct((b, l, b_key), F32),
                   jax.ShapeDtypeStruct(s0.shape, F32)],
        scratch_shapes=[pltpu.VMEM((2, hb, HEAD, HEAD), F32)],
        compiler_params=_cparams("parallel", "parallel", "arbitrary"),
        name=name,
    )(qkv, qkv, qkv, gcol, grow, qkv, qkv, qkv, gcol, grow, pcol, prow, s0)


def _rope(x, cos, s1, s2):
    q = HEAD // 4
    return x * cos + pltpu.roll(x, HEAD - q, 1) * s1 + pltpu.roll(x, q, 1) * s2


def _rope_tables(l):
    quarter = HEAD // 4
    pos = jnp.arange(l)
    inv = ROPE_BASE ** (-jnp.arange(quarter, dtype=F32) / quarter)
    zero = jnp.zeros((l, quarter), F32)
    cos, s1, s2 = [], [], []
    for p in ((pos // GRID_W).astype(F32), (pos % GRID_W).astype(F32)):
        ang = p[:, None] * inv[None, :]
        cos += [jnp.cos(ang), jnp.cos(ang)]
        s1 += [-jnp.sin(ang), zero]
        s2 += [zero, jnp.sin(ang)]
    return (jnp.concatenate(cos, -1), jnp.concatenate(s1, -1), jnp.concatenate(s2, -1))


def _nattn_plan(rows):
    wr = min(WIN_R, rows)
    tr = min(4, rows)
    kbr = min(rows, tr + wr)
    qc = np.arange(GRID_W)
    c0 = np.clip(qc - WIN_C // 2, 0, GRID_W - WIN_C)
    ok_c = (qc[None, :] >= c0[:, None]) & (qc[None, :] < c0[:, None] + WIN_C)
    dc = np.clip(qc[None, :] - qc[:, None] + WIN_C - 1, 0, 2 * WIN_C - 2)
    tiles, classes, keys = [], [], {}
    for t in range(rows // tr):
        ks = int(np.clip(tr * t - wr // 2, 0, rows - kbr))
        qr = tr * t + np.arange(tr)
        kr = ks + np.arange(kbr)
        r0 = np.clip(qr - wr // 2, 0, rows - wr)
        ok_r = (kr[None, :] >= r0[:, None]) & (kr[None, :] < r0[:, None] + wr)
        dr = np.clip(kr[None, :] - qr[:, None] + WIN_R - 1, 0, 2 * WIN_R - 2)
        valid = (ok_r[:, None, :, None] & ok_c[None, :, None, :]).reshape(tr * GRID_W, kbr * GRID_W)
        key = dr.tobytes() + valid.tobytes()
        if key not in keys:
            keys[key] = len(classes)
            classes.append((dr, valid))
        tiles.append((tr * t * GRID_W, ks * GRID_W, keys[key]))
    dr = np.stack([cl[0] for cl in classes]).astype(np.int32)
    valid = np.stack([cl[1] for cl in classes])
    return tuple(tiles), dr, dc.astype(np.int32), valid


def _nattn_bias(rel_bias, dr, dc, valid):
    h = rel_bias.shape[0]
    ncls, tr, kbr = dr.shape
    w = dc.shape[0]
    by_col = jnp.take(rel_bias, dc.reshape(-1), axis=2).reshape(h, -1, w, w)
    tab = jnp.take(by_col, dr.reshape(-1), axis=1).reshape(h, ncls, tr, kbr, w, w)
    tab = jnp.transpose(tab, (0, 1, 2, 4, 3, 5)).reshape(h, ncls, tr * w, kbr * w)
    return jnp.where(valid[None], tab, NEG)


def _nattn_kernel(q_ref, k_ref, v_ref, kc_ref, vc_ref, bias_ref, cos_ref, s1_ref, s2_ref,
                  o_ref, kr_ref, vr_ref, *, tiles):
    tq, tk = bias_ref.shape[2], bias_ref.shape[3]
    kr_ref[...] = _rope(k_ref[0], cos_ref[...], s1_ref[...], s2_ref[...]).astype(BF16)
    vr_ref[...] = v_ref[0].astype(BF16)
    kc = kc_ref[0].astype(BF16)
    vc = vc_ref[0].astype(BF16)
    for q0, k0, cls in tiles:
        qs = slice(q0, q0 + tq)
        qt = _rope(q_ref[0, qs, :], cos_ref[qs, :], s1_ref[qs, :], s2_ref[qs, :]) * HEAD ** -0.5
        qt = qt.astype(BF16)
        s_lat = _dot_nt(qt, kr_ref[k0:k0 + tk, :]) + bias_ref[0, cls]
        s_ctx = _dot_nt(qt, kc)
        m = jnp.maximum(jnp.max(s_lat, axis=-1, keepdims=True), jnp.max(s_ctx, axis=-1, keepdims=True))
        p_lat = jnp.exp(s_lat - m)
        p_ctx = jnp.exp(s_ctx - m)
        den = jnp.sum(p_lat, axis=-1, keepdims=True) + jnp.sum(p_ctx, axis=-1, keepdims=True)
        o = _dot(p_lat, vr_ref[k0:k0 + tk, :]) + _dot(p_ctx, vc)
        o_ref[0, qs, :] = (o / den).astype(o_ref.dtype)


def _nattn(p_lat, p_ctx, bias, tiles, ropes, *, heads, kc_blk, vc_blk, name):
    b, l, _ = p_lat.shape
    lc = p_ctx.shape[1]
    ncls, tq, tk = bias.shape[1:]
    tab = pl.BlockSpec((l, HEAD), lambda h, bi: (0, 0))
    return pl.pallas_call(
        functools.partial(_nattn_kernel, tiles=tiles),
        grid=(heads, b),
        in_specs=[pl.BlockSpec((1, l, HEAD), lambda h, bi: (bi, 0, h)),
                  pl.BlockSpec((1, l, HEAD), lambda h, bi: (bi, 0, heads + h)),
                  pl.BlockSpec((1, l, HEAD), lambda h, bi: (bi, 0, 2 * heads + h)),
                  pl.BlockSpec((1, lc, HEAD), lambda h, bi: (bi, 0, kc_blk + h)),
                  pl.BlockSpec((1, lc, HEAD), lambda h, bi: (bi, 0, vc_blk + h)),
                  pl.BlockSpec((1, ncls, tq, tk), lambda h, bi: (h, 0, 0, 0)),
                  tab, tab, tab],
        out_specs=pl.BlockSpec((1, l, HEAD), lambda h, bi: (bi, 0, h)),
        out_shape=jax.ShapeDtypeStruct((b, l, heads * HEAD), BF16),
        scratch_shapes=[pltpu.VMEM((l, HEAD), BF16), pltpu.VMEM((l, HEAD), BF16)],
        compiler_params=_cparams("parallel", "parallel"),
        name=name,
    )(p_lat, p_lat, p_lat, p_ctx, p_ctx, bias, *ropes)


def _cattn_kernel(q_ref, k_ref, v_ref, o_ref, *, heads):
    for h in range(heads):
        hs = slice(h * HEAD, (h + 1) * HEAD)
        s = _dot_nt(q_ref[0, :, hs] * HEAD ** -0.5, k_ref[0, :, hs])
        p = jnp.exp(s - jnp.max(s, axis=-1, keepdims=True))
        o = _dot(p, v_ref[0, :, hs]) / jnp.sum(p, axis=-1, keepdims=True)
        o_ref[0, :, hs] = o.astype(o_ref.dtype)


def _cattn(p_ctx, *, heads, name):
    b, lc, _ = p_ctx.shape
    blk = (1, lc, heads * HEAD)
    return pl.pallas_call(
        functools.partial(_cattn_kernel, heads=heads),
        grid=(b,),
        in_specs=[pl.BlockSpec(blk, lambda bi: (bi, 0, 0)),
                  pl.BlockSpec(blk, lambda bi: (bi, 0, 1)),
                  pl.BlockSpec(blk, lambda bi: (bi, 0, 2))],
        out_specs=pl.BlockSpec(blk, lambda bi: (bi, 0, 0)),
        out_shape=jax.ShapeDtypeStruct((b, lc, heads * HEAD), BF16),
        compiler_params=_cparams("parallel"),
        name=name,
    )(p_ctx, p_ctx, p_ctx)


def _ffn_up_kernel(x_ref, xp_ref, xn_ref, mod_ref, wa_ref, wg_ref, wdw_ref, bdw_ref, o_ref, xs_ref, *, seq_len):
    tm = x_ref.shape[1]
    i = pl.program_id(1)

    @pl.when(pl.program_id(2) == 0)
    def _():
        sc = 1.0 + mod_ref[0, 4:5, :]
        sh = mod_ref[0, 3:4, :]
        xs_ref[0:HALO, :] = (xp_ref[0] * sc + sh).astype(BF16)
        xs_ref[HALO:HALO + tm, :] = (x_ref[0] * sc + sh).astype(BF16)
        xs_ref[HALO + tm:, :] = (xn_ref[0] * sc + sh).astype(BF16)

    a = jnp.dot(xs_ref[...], wa_ref[...], preferred_element_type=F32)
    gate = jnp.dot(xs_ref[HALO:HALO + tm, :], wg_ref[...], preferred_element_type=F32)
    ext = tm + 2 * HALO
    a_prev = pltpu.roll(a, 1, 0)[HALO:HALO + tm]
    a_next = pltpu.roll(a, ext - 1, 0)[HALO:HALO + tm]
    pos = _seq_pos(i, tm, gate.shape, seq_len)
    a_prev = jnp.where(pos == 0, 0.0, a_prev)
    a_next = jnp.where(pos == seq_len - 1, 0.0, a_next)
    conv = (a_prev * wdw_ref[0:1, :] + a[HALO:HALO + tm] * wdw_ref[1:2, :] + a_next * wdw_ref[2:3, :]
            + bdw_ref[...])
    gelu = 0.5 * conv * (1.0 + lax.erf(conv * np.float32(np.sqrt(0.5))))
    o_ref[0] = (gelu * gate).astype(o_ref.dtype)


def _ffn_up(x, mod, wa, wg, wdw, bdw, *, seq_len, name):
    g, r, k = x.shape
    assert r % seq_len == 0
    f = wa.shape[1]
    tm = _pick(r, (1024, 512, 256, 128, 64, 32, 16))
    tn = _pick(f, (512, 256, 128))
    hpt = tm // HALO
    nh = r // HALO
    gm = mod.shape[0]
    return pl.pallas_call(
        functools.partial(_ffn_up_kernel, seq_len=seq_len),
        grid=(g, r // tm, f // tn),
        in_specs=[pl.BlockSpec((1, tm, k), lambda a, i, j: (a, i, 0)),
                  pl.BlockSpec((1, HALO, k), lambda a, i, j: (a, jnp.maximum(i * hpt - 1, 0), 0)),
                  pl.BlockSpec((1, HALO, k), lambda a, i, j: (a, jnp.minimum((i + 1) * hpt, nh - 1), 0)),
                  pl.BlockSpec((1, 6, k), lambda a, i, j: (a % gm, 0, 0)),
                  pl.BlockSpec((k, tn), lambda a, i, j: (0, j)),
                  pl.BlockSpec((k, tn), lambda a, i, j: (0, j)),
                  pl.BlockSpec((3, tn), lambda a, i, j: (0, j)),
                  pl.BlockSpec((1, tn), lambda a, i, j: (0, j))],
        out_specs=pl.BlockSpec((1, tm, tn), lambda a, i, j: (a, i, j)),
        out_shape=jax.ShapeDtypeStruct((g, r, f), BF16),
        scratch_shapes=[pltpu.VMEM((tm + 2 * HALO, k), BF16)],
        compiler_params=_cparams("parallel", "parallel", "arbitrary"),
        name=name,
    )(x, x, x, mod, wa, wg, wdw, bdw)


def _ffn_down_kernel(x_ref, w_ref, res_ref, mod_ref, g_ref, b_ref, o_ref, *, alpha):
    acc = jnp.dot(x_ref[0], w_ref[...], preferred_element_type=F32)
    r = alpha * res_ref[0] + mod_ref[0, 5:6, :] * acc
    o_ref[0] = _layernorm_rows(r, g_ref[...], b_ref[...])


def _ffn_down(x, w, res, mod, ln_g, ln_b, *, alpha, name):
    g, r, f = x.shape
    d = w.shape[1]
    tm = _pick(r, (256, 128, 64))
    gm = mod.shape[0]
    row = lambda a, i: (a, i, 0)
    fixed = lambda a, i: (0, 0)
    return pl.pallas_call(
        functools.partial(_ffn_down_kernel, alpha=alpha),
        grid=(g, r // tm),
        in_specs=[pl.BlockSpec((1, tm, f), row),
                  pl.BlockSpec((f, d), fixed, pipeline_mode=pl.Buffered(1)),
                  pl.BlockSpec((1, tm, d), row),
                  pl.BlockSpec((1, 6, d), lambda a, i: (a % gm, 0, 0)),
                  pl.BlockSpec((1, d), fixed), pl.BlockSpec((1, d), fixed)],
        out_specs=pl.BlockSpec((1, tm, d), row),
        out_shape=jax.ShapeDtypeStruct((g, r, d), F32),
        compiler_params=_cparams("parallel", "parallel"),
        name=name,
    )(x, w, res, mod, ln_g.reshape(1, d), ln_b.reshape(1, d))


def _pad_cols(w, n):
    return jnp.pad(w, ((0, 0), (0, n - w.shape[1])))


def _flat(a):
    return a.reshape(1, -1, a.shape[-1])


def kernel(x, c, ctx, c_ctx, w_ada, b_ada, ln_g, ln_b, w_in_ab, hgrn_lb, hgrn_norm, gdn_conv, gdn_a_log,
           gdn_dt_bias, gdn_norm, w_out_ab, w_in_c, na_rel_bias, w_out_c, ffn_w_up, ffn_w_dw, ffn_b_dw,
           ffn_w_down):
    b, l, d = x.shape
    lc = ctx.shape[1]
    depth = w_ada.shape[0]
    alpha = (2 * depth) ** 0.25
    a_key = hgrn_lb.shape[-1]
    b_heads = gdn_a_log.shape[-1]
    b_key = b_heads * HEAD
    b_val = gdn_conv.shape[-1] - 2 * b_key
    a_val = w_out_ab.shape[1] - b_val
    n_main = 3 * a_key + 2 * a_val + 2 * b_key + 2 * b_val
    c_heads = d // HEAD
    d_ff = ffn_w_down.shape[1]
    ffp = -(-d_ff // FF_ALIGN) * FF_ALIGN
    assert a_key == a_val == b_key == b_val and w_in_ab.shape[-1] == n_main + 4 * b_heads
    assert l % (4 * GRID_W) == 0 or l // GRID_W < 4

    rows = -(-(b + 1) // 8) * 8
    cc = jnp.concatenate([c, c_ctx[None, :], jnp.zeros((rows - b - 1, d), F32)], axis=0)
    mod = _ada(cc, w_ada, b_ada).reshape(depth, rows, 6, d)

    lb_all = jax.nn.softmax(hgrn_lb.astype(F32), axis=0)
    lb_all = jnp.cumsum(lb_all, axis=0) - lb_all[0]

    tiles, bias_dr, bias_dc, bias_ok = _nattn_plan(l // GRID_W)
    ropes = _rope_tables(l)
    s_zero = jnp.zeros((b, 2, a_key // HEAD, HEAD, HEAD), F32)

    xl, xc = x, ctx
    for layer in range(depth):
        need_ctx = layer < depth - 1
        mod_l = mod[layer, :b]
        mod_c = mod[layer, b:b + 1]
        g0, b0 = ln_g[layer, 0], ln_b[layer, 0]
        if layer % 2 == 0:
            e = layer // 2
            w_main = w_in_ab[e, :, :n_main].astype(BF16)
            w_gate = _pad_cols(w_in_ab[e, :, n_main:], HEAD).astype(BF16)
            w_out = w_out_ab[e].astype(BF16)
            an = hgrn_norm[e].reshape(1, HEAD)
            bn = gdn_norm[e].reshape(1, HEAD)
            gdn_prm = jnp.concatenate([gdn_a_log[e], gdn_dt_bias[e]], axis=0)
            merge = (a_val, b_val, (3 * a_key + a_val) // a_val, (n_main - b_val) // b_val)
            sa, sb = s_zero, s_zero
            for is_ctx, xs, mods in ((True, xc, mod_c), (False, xl, mod_l)):
                tag = f"l{layer}_{'ctx' if is_ctx else 'lat'}"
                ls = xs.shape[1]
                xin = _flat(xs) if is_ctx else xs
                p, gates = _proj_ab(xin, mods, w_main, w_gate, gdn_conv[e], seq_len=ls,
                                    conv_col0=3 * a_key + 2 * a_val, b_key=b_key, name=f"inproj_ab_{tag}")
                p, gates = p.reshape(b, ls, -1), gates.reshape(b, ls, -1)
                oaf, oab, sa = _gla(p, lb_all[e], sa, a_key=a_key, name=f"hgrn2_{tag}")
                obf, obb, sb = _gdn(p, gates, gdn_prm, sb, b_key=b_key,
                                    grp0=(3 * a_key + 2 * a_val) // b_key, name=f"gdn_{tag}")
                if is_ctx and not need_ctx:
                    continue
                y = _outproj((oaf, oab, obf, obb, p, an, bn), w_out, xs, mods,
                             g0, b0, alpha=alpha, merge=merge, name=f"outproj_ab_{tag}")
                if is_ctx:
                    xc = y
                else:
                    xl = y
        else:
            o = layer // 2
            w_in = w_in_c[o].astype(BF16)
            w_out = w_out_c[o].astype(BF16)
            bias = _nattn_bias(na_rel_bias[o], bias_dr, bias_dc, bias_ok)
            p_lat = _proj(xl, mod_l, w_in, shift=0, scale=1, name=f"inproj_c_l{layer}_lat")
            if need_ctx:
                p_ctx = _proj(_flat(xc), mod_c, w_in, shift=0, scale=1, name=f"inproj_c_l{layer}_ctx")
                kc_blk, vc_blk = c_heads, 2 * c_heads
            else:
                p_ctx = _proj(_flat(xc), mod_c, w_in[:, d:], shift=0, scale=1, name=f"inproj_c_l{layer}_ctx")
                kc_blk, vc_blk = 0, c_heads
            p_ctx = p_ctx.reshape(b, lc, -1)
            o_lat = _nattn(p_lat, p_ctx, bias, tiles, ropes, heads=c_heads, kc_blk=kc_blk, vc_blk=vc_blk,
                           name=f"nattn_l{layer}")
            xl = _outproj(o_lat, w_out, xl, mod_l, g0, b0, alpha=alpha, name=f"outproj_c_l{layer}_lat")
            if need_ctx:
                o_ctx = _cattn(p_ctx, heads=c_heads, name=f"cattn_l{layer}")
                xc = _outproj(o_ctx, w_out, xc, mod_c, g0, b0, alpha=alpha, name=f"outproj_c_l{layer}_ctx")

        wa = _pad_cols(ffn_w_up[layer, :, :d_ff], ffp).astype(BF16)
        wg = _pad_cols(ffn_w_up[layer, :, d_ff:], ffp).astype(BF16)
        wdw = _pad_cols(ffn_w_dw[layer], ffp)
        bdw = _pad_cols(ffn_b_dw[layer].reshape(1, d_ff), ffp)
        wdn = jnp.pad(ffn_w_down[layer], ((0, ffp - d_ff), (0, 0))).astype(BF16)
        g1, b1 = ln_g[layer, 1], ln_b[layer, 1]
        for is_ctx, xs, mods in ((True, xc, mod_c), (False, xl, mod_l)):
            if is_ctx and not need_ctx:
                continue
            tag = f"l{layer}_{'ctx' if is_ctx else 'lat'}"
            ls = xs.shape[1]
            hmid = _ffn_up(_flat(xs) if is_ctx else xs, mods, wa, wg, wdw, bdw, seq_len=ls,
                           name=f"ffn_up_{tag}").reshape(b, ls, ffp)
            y = _ffn_down(hmid, wdn, xs, mods, g1, b1, alpha=alpha, name=f"ffn_down_{tag}")
            if is_ctx:
                xc = y
            else:
                xl = y
    return xl
```

```python
import functools

import numpy as np
import jax
import jax.numpy as jnp
from jax import lax
from jax.experimental import pallas as pl
from jax.experimental.pallas import tpu as pltpu

F32 = jnp.float32
BF16 = jnp.bfloat16

HEAD = 128
GRID_W = 64
WIN_R = 8
WIN_C = 16
ROPE_BASE = 10000.0
A_CHUNK = 32
B_CHUNK = 64
EPS = 1e-6
NEG = -1e30
HALO = 16
FF_ALIGN = 512
VMEM_LIMIT = 56 * 1024 * 1024


def _cparams(*sem):
    return pltpu.CompilerParams(dimension_semantics=sem, vmem_limit_bytes=VMEM_LIMIT)


def _pick(n, prefs):
    for p in prefs:
        if n % p == 0:
            return p
    return n


def _sigmoid(x):
    return 1.0 / (1.0 + jnp.exp(-x))


def _silu(x):
    return x * _sigmoid(x)


def _softplus(x):
    return jnp.maximum(x, 0.0) + jnp.log(1.0 + jnp.exp(-jnp.abs(x)))


def _dot(a, b):
    return jnp.dot(a.astype(BF16), b.astype(BF16), preferred_element_type=F32)


def _dot_nt(a, b):
    return lax.dot_general(a.astype(BF16), b.astype(BF16), (((1,), (1,)), ((), ())),
                           preferred_element_type=F32)


def _dot_tn(a, b):
    return lax.dot_general(a.astype(BF16), b.astype(BF16), (((0,), (0,)), ((), ())),
                           preferred_element_type=F32)


def _split2(x):
    hi = x.astype(BF16)
    return hi, (x - hi.astype(F32)).astype(BF16)


def _mm(a, b):
    return jnp.dot(a, b, preferred_element_type=F32)


def _dot3(a, b):
    ah, al = _split2(a)
    bh, bl = _split2(b)
    return _mm(ah, bh) + _mm(ah, bl) + _mm(al, bh)


def _split3(x):
    x1 = x.astype(BF16)
    r = x - x1.astype(F32)
    x2 = r.astype(BF16)
    return x1, x2, (r - x2.astype(F32)).astype(BF16)


def _tri_dot(tri, x):
    return _mm(jnp.concatenate([tri, tri, tri], axis=1), jnp.concatenate(_split3(x), axis=0))


def _dot_tri(x, tri):
    return _mm(jnp.concatenate(_split3(x), axis=1), jnp.concatenate([tri, tri, tri], axis=0))


def _layernorm_rows(r, g, b):
    mu = jnp.mean(r, axis=-1, keepdims=True)
    rc = r - mu
    var = jnp.mean(rc * rc, axis=-1, keepdims=True)
    return rc * lax.rsqrt(var + EPS) * g + b


def _ada_kernel(c_ref, w_ref, b_ref, o_ref):
    s = _silu(c_ref[...])
    o_ref[0] = _dot(s, w_ref[0]) + b_ref[0]


def _ada(cc, w_ada, b_ada):
    depth, d, n = w_ada.shape
    rows = cc.shape[0]
    tn = _pick(n, (1024, 512, 256, 128))
    return pl.pallas_call(
        _ada_kernel,
        grid=(depth, n // tn),
        in_specs=[pl.BlockSpec((rows, d), lambda l, j: (0, 0)),
                  pl.BlockSpec((1, d, tn), lambda l, j: (l, 0, j)),
                  pl.BlockSpec((1, 1, tn), lambda l, j: (l, 0, j))],
        out_specs=pl.BlockSpec((1, rows, tn), lambda l, j: (l, 0, j)),
        out_shape=jax.ShapeDtypeStruct((depth, rows, n), F32),
        compiler_params=_cparams("parallel", "parallel"),
        name="ada_modulation",
    )(cc, w_ada, b_ada.reshape(depth, 1, n))


def _proj_kernel(x_ref, mod_ref, w_ref, o_ref, xs_ref, *, shift, scale):
    @pl.when(pl.program_id(2) == 0)
    def _():
        h = x_ref[0] * (1.0 + mod_ref[0, scale:scale + 1, :]) + mod_ref[0, shift:shift + 1, :]
        xs_ref[...] = h.astype(BF16)

    o_ref[0] = jnp.dot(xs_ref[...], w_ref[...], preferred_element_type=F32).astype(o_ref.dtype)


def _proj(x, mod, w, *, shift, scale, out_dtype=F32, name):
    g, r, k = x.shape
    n = w.shape[1]
    tm = _pick(r, (1024, 512, 256, 128, 64))
    tn = _pick(n, (1024, 512, 256, 128))
    gm = mod.shape[0]
    return pl.pallas_call(
        functools.partial(_proj_kernel, shift=shift, scale=scale),
        grid=(g, r // tm, n // tn),
        in_specs=[pl.BlockSpec((1, tm, k), lambda a, i, j: (a, i, 0)),
                  pl.BlockSpec((1, 6, k), lambda a, i, j: (a % gm, 0, 0)),
                  pl.BlockSpec((k, tn), lambda a, i, j: (0, j))],
        out_specs=pl.BlockSpec((1, tm, tn), lambda a, i, j: (a, i, j)),
        out_shape=jax.ShapeDtypeStruct((g, r, n), out_dtype),
        scratch_shapes=[pltpu.VMEM((tm, k), BF16)],
        compiler_params=_cparams("parallel", "parallel", "arbitrary"),
        name=name,
    )(x, mod, w)


def _seq_pos(i, tm, shape, seq_len):
    return (i * tm + lax.broadcasted_iota(jnp.int32, shape, 0)) % seq_len


def _proj_ab_kernel(x_ref, xp_ref, xn_ref, mod_ref, w_ref, wg_ref, cw_ref, o_ref, og_ref, xs_ref, *,
                    seq_len, conv0, nconv, qk_heads, q_heads, width):
    tm, tn = o_ref.shape[1], o_ref.shape[2]
    i = pl.program_id(1)
    j = pl.program_id(2)

    @pl.when(j == 0)
    def _():
        sc = 1.0 + mod_ref[0, 1:2, :]
        sh = mod_ref[0, 0:1, :]
        xs_ref[0:HALO, :] = (xp_ref[0] * sc + sh).astype(BF16)
        xs_ref[HALO:HALO + tm, :] = (x_ref[0] * sc + sh).astype(BF16)
        xs_ref[HALO + tm:, :] = (xn_ref[0] * sc + sh).astype(BF16)
        og_ref[0] = jnp.dot(xs_ref[HALO:HALO + tm, :], wg_ref[...], preferred_element_type=F32)

    is_conv = (j >= conv0) & (j < conv0 + nconv)

    @pl.when(jnp.logical_not(is_conv))
    def _():
        o_ref[0] = jnp.dot(xs_ref[HALO:HALO + tm, :], w_ref[...], preferred_element_type=F32)

    @pl.when(is_conv)
    def _():
        a = jnp.dot(xs_ref[...], w_ref[...], preferred_element_type=F32)
        ext = tm + 2 * HALO
        pos = _seq_pos(i, tm, (tm, tn), seq_len)
        pad = width // 2
        acc = a[HALO:HALO + tm] * cw_ref[pad:pad + 1, :]
        for t in range(width):
            s = t - pad
            if s == 0:
                continue
            shifted = pltpu.roll(a, (-s) % ext, 0)[HALO:HALO + tm]
            ok = (pos + s >= 0) & (pos + s < seq_len)
            acc = acc + jnp.where(ok, shifted, 0.0) * cw_ref[t:t + 1, :]
        y = _silu(acc)
        hpt = tn // HEAD
        for h in range(hpt):
            hs = slice(h * HEAD, (h + 1) * HEAD)
            head = (j - conv0) * hpt + h
            yh = y[:, hs]
            inv = lax.rsqrt(jnp.sum(yh * yh, axis=-1, keepdims=True) + EPS)
            gain = jnp.where(head < qk_heads, inv, 1.0) * jnp.where(head < q_heads, HEAD ** -0.5, 1.0)
            o_ref[0, :, hs] = yh * gain


def _proj_ab(x, mod, w, wg, conv_w, *, seq_len, conv_col0, b_key, name):
    g, r, k = x.shape
    n = w.shape[1]
    width, cch = conv_w.shape
    tm = _pick(r, (1024, 512, 256, 128, 64))
    tn = _pick(b_key, (1024, 512, 256, 128))
    assert n % tn == 0 and conv_col0 % tn == 0 and cch % tn == 0 and tm % HALO == 0 and r % seq_len == 0
    assert tm % seq_len == 0 or seq_len % tm == 0
    conv0, nconv = conv_col0 // tn, cch // tn
    hpt = tm // HALO
    nh = r // HALO
    gm = mod.shape[0]
    kern = functools.partial(_proj_ab_kernel, seq_len=seq_len, conv0=conv0, nconv=nconv,
                             qk_heads=2 * b_key // HEAD, q_heads=b_key // HEAD, width=width)
    return pl.pallas_call(
        kern,
        grid=(g, r // tm, n // tn),
        in_specs=[pl.BlockSpec((1, tm, k), lambda a, i, j: (a, i, 0)),
                  pl.BlockSpec((1, HALO, k), lambda a, i, j: (a, jnp.maximum(i * hpt - 1, 0), 0)),
                  pl.BlockSpec((1, HALO, k), lambda a, i, j: (a, jnp.minimum((i + 1) * hpt, nh - 1), 0)),
                  pl.BlockSpec((1, 6, k), lambda a, i, j: (a % gm, 0, 0)),
                  pl.BlockSpec((k, tn), lambda a, i, j: (0, j)),
                  pl.BlockSpec((k, HEAD), lambda a, i, j: (0, 0)),
                  pl.BlockSpec((width, tn), lambda a, i, j: (0, jnp.clip(j - conv0, 0, nconv - 1)))],
        out_specs=[pl.BlockSpec((1, tm, tn), lambda a, i, j: (a, i, j)),
                   pl.BlockSpec((1, tm, HEAD), lambda a, i, j: (a, i, 0))],
        out_shape=[jax.ShapeDtypeStruct((g, r, n), F32), jax.ShapeDtypeStruct((g, r, HEAD), F32)],
        scratch_shapes=[pltpu.VMEM((tm + 2 * HALO, k), BF16)],
        compiler_params=_cparams("parallel", "parallel", "arbitrary"),
        name=name,
    )(x, x, x, mod, w, wg, conv_w)


def _merge_half(of_ref, ob_ref, gate_ref, norm_ref, ys_ref, col0, nheads):
    for h in range(nheads):
        hs = slice(h * HEAD, (h + 1) * HEAD)
        o = of_ref[0, :, hs] + ob_ref[0, :, hs]
        ms = jnp.mean(o * o, axis=-1, keepdims=True)
        y = o * lax.rsqrt(ms + EPS) * norm_ref[...] * _silu(gate_ref[0, :, hs])
        ys_ref[:, col0 + h * HEAD:col0 + (h + 1) * HEAD] = y.astype(BF16)


def _outproj_kernel(*refs, merge, alpha, a_heads, b_heads):
    if merge:
        (oaf, oab, obf, obb, ga, zb, an, bn, w_ref, res_ref, mod_ref, g_ref, b_ref, o_ref, ys_ref) = refs
        _merge_half(oaf, oab, ga, an, ys_ref, 0, a_heads)
        _merge_half(obf, obb, zb, bn, ys_ref, a_heads * HEAD, b_heads)
        y = ys_ref[...]
    else:
        x_ref, w_ref, res_ref, mod_ref, g_ref, b_ref, o_ref = refs
        y = x_ref[0]
    acc = jnp.dot(y, w_ref[...], preferred_element_type=F32)
    r = alpha * res_ref[0] + mod_ref[0, 2:3, :] * acc
    o_ref[0] = _layernorm_rows(r, g_ref[...], b_ref[...])


def _outproj(xs, w, res, mod, ln_g, ln_b, *, alpha, merge=None, name):
    g, r, d = res.shape
    k = w.shape[0]
    tm = _pick(r, (512, 256, 128, 64) if merge is None else (256, 128, 64))
    gm = mod.shape[0]
    row = lambda a, i: (a, i, 0)
    fixed = lambda a, i: (0, 0)
    if merge is None:
        in_specs = [pl.BlockSpec((1, tm, k), row)]
        args = [xs]
        scratch = []
        kern = functools.partial(_outproj_kernel, merge=False, alpha=alpha, a_heads=0, b_heads=0)
    else:
        oaf, oab, obf, obb, p, an, bn = xs
        a_val, b_val, ga_blk, zb_blk = merge
        in_specs = [pl.BlockSpec((1, tm, a_val), row), pl.BlockSpec((1, tm, a_val), row),
                    pl.BlockSpec((1, tm, b_val), row), pl.BlockSpec((1, tm, b_val), row),
                    pl.BlockSpec((1, tm, a_val), lambda a, i: (a, i, ga_blk)),
                    pl.BlockSpec((1, tm, b_val), lambda a, i: (a, i, zb_blk)),
                    pl.BlockSpec((1, HEAD), fixed), pl.BlockSpec((1, HEAD), fixed)]
        args = [oaf, oab, obf, obb, p, p, an, bn]
        scratch = [pltpu.VMEM((tm, k), BF16)]
        kern = functools.partial(_outproj_kernel, merge=True, alpha=alpha,
                                 a_heads=a_val // HEAD, b_heads=b_val // HEAD)
    in_specs += [pl.BlockSpec((k, d), fixed, pipeline_mode=pl.Buffered(1)),
                 pl.BlockSpec((1, tm, d), row),
                 pl.BlockSpec((1, 6, d), lambda a, i: (a % gm, 0, 0)),
                 pl.BlockSpec((1, d), fixed), pl.BlockSpec((1, d), fixed)]
    args += [w, res, mod, ln_g.reshape(1, d), ln_b.reshape(1, d)]
    return pl.pallas_call(
        kern,
        grid=(g, r // tm),
        in_specs=in_specs,
        out_specs=pl.BlockSpec((1, tm, d), row),
        out_shape=jax.ShapeDtypeStruct((g, r, d), F32),
        scratch_shapes=scratch,
        compiler_params=_cparams("parallel", "parallel"),
        name=name,
    )(*args)


def _tri_masks(n):
    row = lax.broadcasted_iota(jnp.int32, (n, n), 0)
    col = lax.broadcasted_iota(jnp.int32, (n, n), 1)
    return col <= row, col >= row


def _gla_kernel(qf_ref, zf_ref, vf_ref, qb_ref, zb_ref, vb_ref, lb_ref, s0_ref,
                of_ref, ob_ref, sfin_ref, st_ref, *, hb, nchunk):
    t = pl.program_id(2)
    c = A_CHUNK

    @pl.when(t == 0)
    def _():
        st_ref[...] = s0_ref[0]

    causal, anti = _tri_masks(c)
    streams = ((qf_ref, zf_ref, vf_ref, of_ref, causal, c // 2 - 1, c - 1),
               (qb_ref, zb_ref, vb_ref, ob_ref, anti, c - c // 2, 0))

    def body(ci, carry):
        items = []
        for d, (q_ref, z_ref, v_ref, o_ref, mask, mid, last) in enumerate(streams):
            cc = ci if d == 0 else nchunk - 1 - ci
            sl = pl.ds(pl.multiple_of(cc * c, c), c)
            q = _silu(q_ref[0, sl, :])
            v = v_ref[0, sl, :]
            lb = lb_ref[d:d + 1, :]
            f = lb + (1.0 - lb) * _sigmoid(z_ref[0, sl, :])
            k = 1.0 - f
            cum = _tri_dot(mask.astype(BF16), jnp.log(f))
            m = cum[mid:mid + 1, :]
            bl = cum[last:last + 1, :]
            qm = q * jnp.exp(cum - m)
            km = k * jnp.exp(m - cum)
            qe = qm * jnp.exp(m)
            kd = km * jnp.exp(bl - m)
            dec = jnp.exp(bl)
            for h in range(hb):
                hs = slice(h * HEAD, (h + 1) * HEAD)
                items.append(dict(d=d, h=h, sl=sl, hs=hs, o_ref=o_ref, mask=mask, qe=qe[:, hs], qm=qm[:, hs],
                                  km=km[:, hs], kd=kd[:, hs], v=v[:, hs], dec=dec[:, hs]))
        for it in items:
            it["a"] = jnp.where(it["mask"], _dot_nt(it["qm"], it["km"]), 0.0)
            it["u"] = _dot_tn(it["v"], it["kd"])
        for it in items:
            it["st"] = st_ref[it["d"], it["h"]]
            it["o"] = _dot_nt(it["qe"], it["st"]) + _dot(it["a"], it["v"])
        for it in items:
            it["o_ref"][0, it["sl"], it["hs"]] = it["o"]
            st_ref[it["d"], it["h"]] = it["st"] * it["dec"] + it["u"]
        return carry

    lax.fori_loop(0, nchunk, body, 0, unroll=4)

    @pl.when(t == pl.num_programs(2) - 1)
    def _():
        sfin_ref[0] = st_ref[...]


def _gla(p, lb, s0, *, a_key, name):
    b, l, _ = p.shape
    heads = a_key // HEAD
    hb = _pick(heads, (8, 4, 2, 1))
    wb = hb * HEAD
    per = a_key // wb
    tb = _pick(l, (256, 128, 64, 32))
    nblk = l // tb
    fwd = lambda grp: (lambda bi, hg, t: (bi, t, grp * per + hg))
    bwd = lambda grp: (lambda bi, hg, t: (bi, nblk - 1 - t, grp * per + hg))
    blk = (1, tb, wb)
    st_blk = (1, 2, hb, HEAD, HEAD)
    st_map = lambda bi, hg, t: (bi, 0, hg, 0, 0)
    return pl.pallas_call(
        functools.partial(_gla_kernel, hb=hb, nchunk=tb // A_CHUNK),
        grid=(b, heads // hb, nblk),
        in_specs=[pl.BlockSpec(blk, fwd(0)), pl.BlockSpec(blk, fwd(1)), pl.BlockSpec(blk, fwd(3)),
                  pl.BlockSpec(blk, bwd(0)), pl.BlockSpec(blk, bwd(2)), pl.BlockSpec(blk, bwd(3)),
                  pl.BlockSpec((2, wb), lambda bi, hg, t: (0, hg)),
                  pl.BlockSpec(st_blk, st_map)],
        out_specs=[pl.BlockSpec(blk, lambda bi, hg, t: (bi, t, hg)),
                   pl.BlockSpec(blk, lambda bi, hg, t: (bi, nblk - 1 - t, hg)),
                   pl.BlockSpec(st_blk, st_map)],
        out_shape=[jax.ShapeDtypeStruct((b, l, a_key), F32),
                   jax.ShapeDtypeStruct((b, l, a_key), F32),
                   jax.ShapeDtypeStruct(s0.shape, F32)],
        scratch_shapes=[pltpu.VMEM((2, hb, HEAD, HEAD), F32)],
        compiler_params=_cparams("parallel", "parallel", "arbitrary"),
        name=name,
    )(p, p, p, p, p, p, lb, s0)


def _gdn_kernel(qf_ref, kf_ref, vf_ref, gcf_ref, grf_ref, qb_ref, kb_ref, vb_ref, gcb_ref, grb_ref,
                pc_ref, pr_ref, s0_ref, of_ref, ob_ref, sfin_ref, st_ref, *, hb, nchunk):
    t = pl.program_id(2)
    c = B_CHUNK

    @pl.when(t == 0)
    def _():
        st_ref[...] = s0_ref[0]

    row = lax.broadcasted_iota(jnp.int32, (c, HEAD), 0)
    col = lax.broadcasted_iota(jnp.int32, (c, HEAD), 1)
    right = col >= c
    eye_right = (col == row + c).astype(F32)
    causal, anti = _tri_masks(c)
    zeros_k = jnp.zeros((c, HEAD), F32)
    dirs = ((col <= row, col < row, causal.astype(BF16), ((col >= row) & ~right).astype(BF16), c - 1),
            ((col >= row) & ~right, (col > row) & ~right, anti.astype(BF16), (col <= row).astype(BF16), 0))
    streams = ((qf_ref, kf_ref, vf_ref, gcf_ref, grf_ref, of_ref), (qb_ref, kb_ref, vb_ref, gcb_ref, grb_ref, ob_ref))

    def body(ci, carry):
        items = []
        for d in range(2):
            q_ref, k_ref, v_ref, gc_ref, gr_ref, o_ref = streams[d]
            incl, strict, tri_c, tri_r, last = dirs[d]
            cc = ci if d == 0 else nchunk - 1 - ci
            sl = pl.ds(pl.multiple_of(cc * c, c), c)
            gcol = gc_ref[0, 0, sl, :]
            a_log = pr_ref[0, d:d + 1, 0:hb]
            dt = pr_ref[0, 2 + d:3 + d, 0:hb]
            g_col = _tri_dot(tri_c, -jnp.exp(a_log) * _softplus(gcol[:, d * hb:(d + 1) * hb] + dt))
            beta_col = _sigmoid(gcol[:, (2 + d) * hb:(3 + d) * hb])
            grow = gr_ref[0, 0, cc]
            a_log_c = pc_ref[0, :, d:d + 1]
            dt_c = pc_ref[0, :, 2 + d:3 + d]
            g_row = _dot_tri(-jnp.exp(a_log_c) * _softplus(grow[d * hb:(d + 1) * hb, :] + dt_c), tri_r)
            for hl in range(hb):
                hs = slice(hl * HEAD, (hl + 1) * HEAD)
                items.append(dict(d=d, hl=hl, sl=sl, hs=hs, o_ref=o_ref, incl=incl, strict=strict, last=last,
                                  gc=g_col[:, hl:hl + 1], bc=beta_col[:, hl:hl + 1], gr=g_row[hl:hl + 1, :],
                                  q=q_ref[0, sl, hs], k=k_ref[0, sl, hs], v=v_ref[0, sl, hs]))
        for it in items:
            it["decay"] = jnp.exp(jnp.where(it["incl"], it["gc"] - it["gr"], NEG))
            it["kb"] = it["k"] * it["bc"]
        for it in items:
            x = jnp.concatenate([it["kb"], it["q"]], axis=0)
            kpad = jnp.concatenate([it["k"], zeros_k], axis=0)
            it["kq"] = _dot_nt(x, kpad)
        for it in items:
            it["w"] = eye_right - jnp.where(it["strict"], it["kq"][:c] * it["decay"], 0.0)
            it["attn"] = it["kq"][c:] * it["decay"]
        for _ in range(int(np.log2(c))):
            rs = [_dot3(it["w"][:, :c], it["w"]) for it in items]
            for it, r in zip(items, rs):
                it["w"] = r + jnp.where(right, it["w"], 0.0)
        for it in items:
            tinv = pltpu.roll(it["w"], c, 1)[:, :c]
            egc = jnp.exp(it["gc"])
            rhs = jnp.concatenate([it["v"] * it["bc"], it["kb"] * egc], axis=1)
            it["sol"] = _dot3(tinv, rhs)
            g_last = it["gc"][it["last"]:it["last"] + 1, :]
            it["qe"] = it["q"] * egc
            it["kdt"] = jnp.transpose(it["k"] * jnp.exp(g_last - it["gc"]))
            it["dlast"] = jnp.exp(g_last)
        for it in items:
            it["s"] = st_ref[it["d"], it["hl"]]
            it["r"] = _dot(jnp.concatenate([it["sol"][:, HEAD:], it["qe"]], axis=0), it["s"])
        for it in items:
            it["vn"] = it["sol"][:, :HEAD] - it["r"][:c]
            it["r2"] = _dot(jnp.concatenate([it["attn"][:, :c], it["kdt"]], axis=0), it["vn"])
        for it in items:
            it["o_ref"][0, it["sl"], it["hs"]] = it["r"][c:] + it["r2"][:c]
            st_ref[it["d"], it["hl"]] = it["s"] * it["dlast"] + it["r2"][c:]
        return carry

    lax.fori_loop(0, nchunk, body, 0, unroll=2)

    @pl.when(t == pl.num_programs(2) - 1)
    def _():
        sfin_ref[0] = st_ref[...]


def _gdn(qkv, gates, prm, s0, *, b_key, grp0, name):
    b, l, _ = qkv.shape
    assert HEAD == 2 * B_CHUNK
    heads = b_key // HEAD
    hb = _pick(heads, (8, 4, 2, 1))
    nhg = heads // hb
    wb = hb * HEAD
    tb = _pick(l, (256, 128, 64))
    nblk = l // tb
    cpb = tb // B_CHUNK
    g4 = gates[:, :, :4 * heads].reshape(b, l, 4, nhg, hb)
    gcol = jnp.transpose(g4, (0, 3, 1, 2, 4)).reshape(b, nhg, l, 4 * hb)
    grow = jnp.swapaxes(gcol.reshape(b, nhg, l // B_CHUNK, B_CHUNK, 4 * hb), 3, 4)
    gcol = jnp.pad(gcol, ((0, 0), (0, 0), (0, 0), (0, HEAD - 4 * hb)))
    p4 = prm.reshape(4, nhg, hb)
    prow = jnp.pad(jnp.transpose(p4, (1, 0, 2)), ((0, 0), (0, 0), (0, HEAD - hb)))
    pcol = jnp.pad(jnp.transpose(p4, (1, 2, 0)), ((0, 0), (0, 0), (0, HEAD - 4)))
    fwd = lambda grp: (lambda bi, hg, t: (bi, t, (grp0 + grp) * nhg + hg))
    bwd = lambda grp: (lambda bi, hg, t: (bi, nblk - 1 - t, (grp0 + grp) * nhg + hg))
    blk = (1, tb, wb)
    gc_blk = (1, 1, tb, HEAD)
    gr_blk = (1, 1, cpb, 4 * hb, B_CHUNK)
    st_blk = (1, 2, hb, HEAD, HEAD)
    st_map = lambda bi, hg, t: (bi, 0, hg, 0, 0)
    return pl.pallas_call(
        functools.partial(_gdn_kernel, hb=hb, nchunk=cpb),
        grid=(b, nhg, nblk),
        in_specs=[pl.BlockSpec(blk, fwd(0)), pl.BlockSpec(blk, fwd(1)), pl.BlockSpec(blk, fwd(2)),
                  pl.BlockSpec(gc_blk, lambda bi, hg, t: (bi, hg, t, 0)),
                  pl.BlockSpec(gr_blk, lambda bi, hg, t: (bi, hg, t, 0, 0)),
                  pl.BlockSpec(blk, bwd(0)), pl.BlockSpec(blk, bwd(1)), pl.BlockSpec(blk, bwd(2)),
                  pl.BlockSpec(gc_blk, lambda bi, hg, t: (bi, hg, nblk - 1 - t, 0)),
                  pl.BlockSpec(gr_blk, lambda bi, hg, t: (bi, hg, nblk - 1 - t, 0, 0)),
                  pl.BlockSpec((1, hb, HEAD), lambda bi, hg, t: (hg, 0, 0)),
                  pl.BlockSpec((1, 4, HEAD), lambda bi, hg, t: (hg, 0, 0)),
                  pl.BlockSpec(st_blk, st_map)],
        out_specs=[pl.BlockSpec(blk, lambda bi, hg, t: (bi, t, hg)),
                   pl.BlockSpec(blk, lambda bi, hg, t: (bi, nblk - 1 - t, hg)),
                   pl.BlockSpec(st_blk, st_map)],
        out_shape=[jax.ShapeDtypeStruct((b, l, b_key), F32),
                   jax.ShapeDtypeStruct((b, l, b_key), F32),
                   jax.ShapeDtypeStruct(s0.shape, F32)],
        scratch_shapes=[pltpu.VMEM((2, hb, HEAD, HEAD), F32)],
        compiler_params=_cparams("parallel", "parallel", "arbitrary"),
        name=name,
    )(qkv, qkv, qkv, gcol, grow, qkv, qkv, qkv, gcol, grow, pcol, prow, s0)


def _rope(x, cos, s1, s2):
    q = HEAD // 4
    return x * cos + pltpu.roll(x, HEAD - q, 1) * s1 + pltpu.roll(x, q, 1) * s2


def _rope_tables(l):
    quarter = HEAD // 4
    pos = jnp.arange(l)
    inv = ROPE_BASE ** (-jnp.arange(quarter, dtype=F32) / quarter)
    zero = jnp.zeros((l, quarter), F32)
    cos, s1, s2 = [], [], []
    for p in ((pos // GRID_W).astype(F32), (pos % GRID_W).astype(F32)):
        ang = p[:, None] * inv[None, :]
        cos += [jnp.cos(ang), jnp.cos(ang)]
        s1 += [-jnp.sin(ang), zero]
        s2 += [zero, jnp.sin(ang)]
    return (jnp.concatenate(cos, -1), jnp.concatenate(s1, -1), jnp.concatenate(s2, -1))


def _nattn_plan(rows):
    wr = min(WIN_R, rows)
    tr = min(4, rows)
    kbr = min(rows, tr + wr)
    qc = np.arange(GRID_W)
    c0 = np.clip(qc - WIN_C // 2, 0, GRID_W - WIN_C)
    ok_c = (qc[None, :] >= c0[:, None]) & (qc[None, :] < c0[:, None] + WIN_C)
    dc = np.clip(qc[None, :] - qc[:, None] + WIN_C - 1, 0, 2 * WIN_C - 2)
    tiles, classes, keys = [], [], {}
    for t in range(rows // tr):
        ks = int(np.clip(tr * t - wr // 2, 0, rows - kbr))
        qr = tr * t + np.arange(tr)
        kr = ks + np.arange(kbr)
        r0 = np.clip(qr - wr // 2, 0, rows - wr)
        ok_r = (kr[None, :] >= r0[:, None]) & (kr[None, :] < r0[:, None] + wr)
        dr = np.clip(kr[None, :] - qr[:, None] + WIN_R - 1, 0, 2 * WIN_R - 2)
        valid = (ok_r[:, None, :, None] & ok_c[None, :, None, :]).reshape(tr * GRID_W, kbr * GRID_W)
        key = dr.tobytes() + valid.tobytes()
        if key not in keys:
            keys[key] = len(classes)
            classes.append((dr, valid))
        tiles.append((tr * t * GRID_W, ks * GRID_W, keys[key]))
    dr = np.stack([cl[0] for cl in classes]).astype(np.int32)
    valid = np.stack([cl[1] for cl in classes])
    return tuple(tiles), dr, dc.astype(np.int32), valid


def _nattn_bias(rel_bias, dr, dc, valid):
    h = rel_bias.shape[0]
    ncls, tr, kbr = dr.shape
    w = dc.shape[0]
    by_col = jnp.take(rel_bias, dc.reshape(-1), axis=2).reshape(h, -1, w, w)
    tab = jnp.take(by_col, dr.reshape(-1), axis=1).reshape(h, ncls, tr, kbr, w, w)
    tab = jnp.transpose(tab, (0, 1, 2, 4, 3, 5)).reshape(h, ncls, tr * w, kbr * w)
    return jnp.where(valid[None], tab, NEG)


def _nattn_kernel(q_ref, k_ref, v_ref, kc_ref, vc_ref, bias_ref, cos_ref, s1_ref, s2_ref,
                  o_ref, kr_ref, vr_ref, *, tiles):
    tq, tk = bias_ref.shape[2], bias_ref.shape[3]
    kr_ref[...] = _rope(k_ref[0], cos_ref[...], s1_ref[...], s2_ref[...]).astype(BF16)
    vr_ref[...] = v_ref[0].astype(BF16)
    kc = kc_ref[0].astype(BF16)
    vc = vc_ref[0].astype(BF16)
    for q0, k0, cls in tiles:
        qs = slice(q0, q0 + tq)
        qt = _rope(q_ref[0, qs, :], cos_ref[qs, :], s1_ref[qs, :], s2_ref[qs, :]) * HEAD ** -0.5
        qt = qt.astype(BF16)
        s_lat = _dot_nt(qt, kr_ref[k0:k0 + tk, :]) + bias_ref[0, cls]
        s_ctx = _dot_nt(qt, kc)
        m = jnp.maximum(jnp.max(s_lat, axis=-1, keepdims=True), jnp.max(s_ctx, axis=-1, keepdims=True))
        p_lat = jnp.exp(s_lat - m)
        p_ctx = jnp.exp(s_ctx - m)
        den = jnp.sum(p_lat, axis=-1, keepdims=True) + jnp.sum(p_ctx, axis=-1, keepdims=True)
        o = _dot(p_lat, vr_ref[k0:k0 + tk, :]) + _dot(p_ctx, vc)
        o_ref[0, qs, :] = (o / den).astype(o_ref.dtype)


def _nattn(p_lat, p_ctx, bias, tiles, ropes, *, heads, kc_blk, vc_blk, name):
    b, l, _ = p_lat.shape
    lc = p_ctx.shape[1]
    ncls, tq, tk = bias.shape[1:]
    tab = pl.BlockSpec((l, HEAD), lambda h, bi: (0, 0))
    return pl.pallas_call(
        functools.partial(_nattn_kernel, tiles=tiles),
        grid=(heads, b),
        in_specs=[pl.BlockSpec((1, l, HEAD), lambda h, bi: (bi, 0, h)),
                  pl.BlockSpec((1, l, HEAD), lambda h, bi: (bi, 0, heads + h)),
                  pl.BlockSpec((1, l, HEAD), lambda h, bi: (bi, 0, 2 * heads + h)),
                  pl.BlockSpec((1, lc, HEAD), lambda h, bi: (bi, 0, kc_blk + h)),
                  pl.BlockSpec((1, lc, HEAD), lambda h, bi: (bi, 0, vc_blk + h)),
                  pl.BlockSpec((1, ncls, tq, tk), lambda h, bi: (h, 0, 0, 0)),
                  tab, tab, tab],
        out_specs=pl.BlockSpec((1, l, HEAD), lambda h, bi: (bi, 0, h)),
        out_shape=jax.ShapeDtypeStruct((b, l, heads * HEAD), BF16),
        scratch_shapes=[pltpu.VMEM((l, HEAD), BF16), pltpu.VMEM((l, HEAD), BF16)],
        compiler_params=_cparams("parallel", "parallel"),
        name=name,
    )(p_lat, p_lat, p_lat, p_ctx, p_ctx, bias, *ropes)


def _cattn_kernel(q_ref, k_ref, v_ref, o_ref, *, heads):
    for h in range(heads):
        hs = slice(h * HEAD, (h + 1) * HEAD)
        s = _dot_nt(q_ref[0, :, hs] * HEAD ** -0.5, k_ref[0, :, hs])
        p = jnp.exp(s - jnp.max(s, axis=-1, keepdims=True))
        o = _dot(p, v_ref[0, :, hs]) / jnp.sum(p, axis=-1, keepdims=True)
        o_ref[0, :, hs] = o.astype(o_ref.dtype)


def _cattn(p_ctx, *, heads, name):
    b, lc, _ = p_ctx.shape
    blk = (1, lc, heads * HEAD)
    return pl.pallas_call(
        functools.partial(_cattn_kernel, heads=heads),
        grid=(b,),
        in_specs=[pl.BlockSpec(blk, lambda bi: (bi, 0, 0)),
                  pl.BlockSpec(blk, lambda bi: (bi, 0, 1)),
                  pl.BlockSpec(blk, lambda bi: (bi, 0, 2))],
        out_specs=pl.BlockSpec(blk, lambda bi: (bi, 0, 0)),
        out_shape=jax.ShapeDtypeStruct((b, lc, heads * HEAD), BF16),
        compiler_params=_cparams("parallel"),
        name=name,
    )(p_ctx, p_ctx, p_ctx)


def _ffn_up_kernel(x_ref, xp_ref, xn_ref, mod_ref, wa_ref, wg_ref, wdw_ref, bdw_ref, o_ref, xs_ref, *, seq_len):
    tm = x_ref.shape[1]
    i = pl.program_id(1)

    @pl.when(pl.program_id(2) == 0)
    def _():
        sc = 1.0 + mod_ref[0, 4:5, :]
        sh = mod_ref[0, 3:4, :]
        xs_ref[0:HALO, :] = (xp_ref[0] * sc + sh).astype(BF16)
        xs_ref[HALO:HALO + tm, :] = (x_ref[0] * sc + sh).astype(BF16)
        xs_ref[HALO + tm:, :] = (xn_ref[0] * sc + sh).astype(BF16)

    a = jnp.dot(xs_ref[...], wa_ref[...], preferred_element_type=F32)
    gate = jnp.dot(xs_ref[HALO:HALO + tm, :], wg_ref[...], preferred_element_type=F32)
    ext = tm + 2 * HALO
    a_prev = pltpu.roll(a, 1, 0)[HALO:HALO + tm]
    a_next = pltpu.roll(a, ext - 1, 0)[HALO:HALO + tm]
    pos = _seq_pos(i, tm, gate.shape, seq_len)
    a_prev = jnp.where(pos == 0, 0.0, a_prev)
    a_next = jnp.where(pos == seq_len - 1, 0.0, a_next)
    conv = (a_prev * wdw_ref[0:1, :] + a[HALO:HALO + tm] * wdw_ref[1:2, :] + a_next * wdw_ref[2:3, :]
            + bdw_ref[...])
    gelu = 0.5 * conv * (1.0 + lax.erf(conv * np.float32(np.sqrt(0.5))))
    o_ref[0] = (gelu * gate).astype(o_ref.dtype)


def _ffn_up(x, mod, wa, wg, wdw, bdw, *, seq_len, name):
    g, r, k = x.shape
    assert r % seq_len == 0
    f = wa.shape[1]
    tm = _pick(r, (1024, 512, 256, 128, 64, 32, 16))
    tn = _pick(f, (512, 256, 128))
    hpt = tm // HALO
    nh = r // HALO
    gm = mod.shape[0]
    return pl.pallas_call(
        functools.partial(_ffn_up_kernel, seq_len=seq_len),
        grid=(g, r // tm, f // tn),
        in_specs=[pl.BlockSpec((1, tm, k), lambda a, i, j: (a, i, 0)),
                  pl.BlockSpec((1, HALO, k), lambda a, i, j: (a, jnp.maximum(i * hpt - 1, 0), 0)),
                  pl.BlockSpec((1, HALO, k), lambda a, i, j: (a, jnp.minimum((i + 1) * hpt, nh - 1), 0)),
                  pl.BlockSpec((1, 6, k), lambda a, i, j: (a % gm, 0, 0)),
                  pl.BlockSpec((k, tn), lambda a, i, j: (0, j)),
                  pl.BlockSpec((k, tn), lambda a, i, j: (0, j)),
                  pl.BlockSpec((3, tn), lambda a, i, j: (0, j)),
                  pl.BlockSpec((1, tn), lambda a, i, j: (0, j))],
        out_specs=pl.BlockSpec((1, tm, tn), lambda a, i, j: (a, i, j)),
        out_shape=jax.ShapeDtypeStruct((g, r, f), BF16),
        scratch_shapes=[pltpu.VMEM((tm + 2 * HALO, k), BF16)],
        compiler_params=_cparams("parallel", "parallel", "arbitrary"),
        name=name,
    )(x, x, x, mod, wa, wg, wdw, bdw)


def _ffn_down_kernel(x_ref, w_ref, res_ref, mod_ref, g_ref, b_ref, o_ref, *, alpha):
    acc = jnp.dot(x_ref[0], w_ref[...], preferred_element_type=F32)
    r = alpha * res_ref[0] + mod_ref[0, 5:6, :] * acc
    o_ref[0] = _layernorm_rows(r, g_ref[...], b_ref[...])


def _ffn_down(x, w, res, mod, ln_g, ln_b, *, alpha, name):
    g, r, f = x.shape
    d = w.shape[1]
    tm = _pick(r, (256, 128, 64))
    gm = mod.shape[0]
    row = lambda a, i: (a, i, 0)
    fixed = lambda a, i: (0, 0)
    return pl.pallas_call(
        functools.partial(_ffn_down_kernel, alpha=alpha),
        grid=(g, r // tm),
        in_specs=[pl.BlockSpec((1, tm, f), row),
                  pl.BlockSpec((f, d), fixed, pipeline_mode=pl.Buffered(1)),
                  pl.BlockSpec((1, tm, d), row),
                  pl.BlockSpec((1, 6, d), lambda a, i: (a % gm, 0, 0)),
                  pl.BlockSpec((1, d), fixed), pl.BlockSpec((1, d), fixed)],
        out_specs=pl.BlockSpec((1, tm, d), row),
        out_shape=jax.ShapeDtypeStruct((g, r, d), F32),
        compiler_params=_cparams("parallel", "parallel"),
        name=name,
    )(x, w, res, mod, ln_g.reshape(1, d), ln_b.reshape(1, d))


def _pad_cols(w, n):
    return jnp.pad(w, ((0, 0), (0, n - w.shape[1])))


def _flat(a):
    return a.reshape(1, -1, a.shape[-1])


def kernel(x, c, ctx, c_ctx, w_ada, b_ada, ln_g, ln_b, w_in_ab, hgrn_lb, hgrn_norm, gdn_conv, gdn_a_log,
           gdn_dt_bias, gdn_norm, w_out_ab, w_in_c, na_rel_bias, w_out_c, ffn_w_up, ffn_w_dw, ffn_b_dw,
           ffn_w_down):
    b, l, d = x.shape
    lc = ctx.shape[1]
    depth = w_ada.shape[0]
    alpha = (2 * depth) ** 0.25
    a_key = hgrn_lb.shape[-1]
    b_heads = gdn_a_log.shape[-1]
    b_key = b_heads * HEAD
    b_val = gdn_conv.shape[-1] - 2 * b_key
    a_val = w_out_ab.shape[1] - b_val
    n_main = 3 * a_key + 2 * a_val + 2 * b_key + 2 * b_val
    c_heads = d // HEAD
    d_ff = ffn_w_down.shape[1]
    ffp = -(-d_ff // FF_ALIGN) * FF_ALIGN
    assert a_key == a_val == b_key == b_val and w_in_ab.shape[-1] == n_main + 4 * b_heads
    assert l % (4 * GRID_W) == 0 or l // GRID_W < 4

    rows = -(-(b + 1) // 8) * 8
    cc = jnp.concatenate([c, c_ctx[None, :], jnp.zeros((rows - b - 1, d), F32)], axis=0)
    mod = _ada(cc, w_ada, b_ada).reshape(depth, rows, 6, d)

    lb_all = jax.nn.softmax(hgrn_lb.astype(F32), axis=0)
    lb_all = jnp.cumsum(lb_all, axis=0) - lb_all[0]

    tiles, bias_dr, bias_dc, bias_ok = _nattn_plan(l // GRID_W)
    ropes = _rope_tables(l)
    s_zero = jnp.zeros((b, 2, a_key // HEAD, HEAD, HEAD), F32)

    xl, xc = x, ctx
    for layer in range(depth):
        need_ctx = layer < depth - 1
        mod_l = mod[layer, :b]
        mod_c = mod[layer, b:b + 1]
        g0, b0 = ln_g[layer, 0], ln_b[layer, 0]
        if layer % 2 == 0:
            e = layer // 2
            w_main = w_in_ab[e, :, :n_main].astype(BF16)
            w_gate = _pad_cols(w_in_ab[e, :, n_main:], HEAD).astype(BF16)
            w_out = w_out_ab[e].astype(BF16)
            an = hgrn_norm[e].reshape(1, HEAD)
            bn = gdn_norm[e].reshape(1, HEAD)
            gdn_prm = jnp.concatenate([gdn_a_log[e], gdn_dt_bias[e]], axis=0)
            merge = (a_val, b_val, (3 * a_key + a_val) // a_val, (n_main - b_val) // b_val)
            sa, sb = s_zero, s_zero
            for is_ctx, xs, mods in ((True, xc, mod_c), (False, xl, mod_l)):
                tag = f"l{layer}_{'ctx' if is_ctx else 'lat'}"
                ls = xs.shape[1]
                xin = _flat(xs) if is_ctx else xs
                p, gates = _proj_ab(xin, mods, w_main, w_gate, gdn_conv[e], seq_len=ls,
                                    conv_col0=3 * a_key + 2 * a_val, b_key=b_key, name=f"inproj_ab_{tag}")
                p, gates = p.reshape(b, ls, -1), gates.reshape(b, ls, -1)
                oaf, oab, sa = _gla(p, lb_all[e], sa, a_key=a_key, name=f"hgrn2_{tag}")
                obf, obb, sb = _gdn(p, gates, gdn_prm, sb, b_key=b_key,
                                    grp0=(3 * a_key + 2 * a_val) // b_key, name=f"gdn_{tag}")
                if is_ctx and not need_ctx:
                    continue
                y = _outproj((oaf, oab, obf, obb, p, an, bn), w_out, xs, mods,
                             g0, b0, alpha=alpha, merge=merge, name=f"outproj_ab_{tag}")
                if is_ctx:
                    xc = y
                else:
                    xl = y
        else:
            o = layer // 2
            w_in = w_in_c[o].astype(BF16)
            w_out = w_out_c[o].astype(BF16)
            bias = _nattn_bias(na_rel_bias[o], bias_dr, bias_dc, bias_ok)
            p_lat = _proj(xl, mod_l, w_in, shift=0, scale=1, name=f"inproj_c_l{layer}_lat")
            if need_ctx:
                p_ctx = _proj(_flat(xc), mod_c, w_in, shift=0, scale=1, name=f"inproj_c_l{layer}_ctx")
                kc_blk, vc_blk = c_heads, 2 * c_heads
            else:
                p_ctx = _proj(_flat(xc), mod_c, w_in[:, d:], shift=0, scale=1, name=f"inproj_c_l{layer}_ctx")
                kc_blk, vc_blk = 0, c_heads
            p_ctx = p_ctx.reshape(b, lc, -1)
            o_lat = _nattn(p_lat, p_ctx, bias, tiles, ropes, heads=c_heads, kc_blk=kc_blk, vc_blk=vc_blk,
                           name=f"nattn_l{layer}")
            xl = _outproj(o_lat, w_out, xl, mod_l, g0, b0, alpha=alpha, name=f"outproj_c_l{layer}_lat")
            if need_ctx:
                o_ctx = _cattn(p_ctx, heads=c_heads, name=f"cattn_l{layer}")
                xc = _outproj(o_ctx, w_out, xc, mod_c, g0, b0, alpha=alpha, name=f"outproj_c_l{layer}_ctx")

        wa = _pad_cols(ffn_w_up[layer, :, :d_ff], ffp).astype(BF16)
        wg = _pad_cols(ffn_w_up[layer, :, d_ff:], ffp).astype(BF16)
        wdw = _pad_cols(ffn_w_dw[layer], ffp)
        bdw = _pad_cols(ffn_b_dw[layer].reshape(1, d_ff), ffp)
        wdn = jnp.pad(ffn_w_down[layer], ((0, ffp - d_ff), (0, 0))).astype(BF16)
        g1, b1 = ln_g[layer, 1], ln_b[layer, 1]
        for is_ctx, xs, mods in ((True, xc, mod_c), (False, xl, mod_l)):
            if is_ctx and not need_ctx:
                continue
            tag = f"l{layer}_{'ctx' if is_ctx else 'lat'}"
            ls = xs.shape[1]
            hmid = _ffn_up(_flat(xs) if is_ctx else xs, mods, wa, wg, wdw, bdw, seq_len=ls,
                           name=f"ffn_up_{tag}").reshape(b, ls, ffp)
            y = _ffn_down(hmid, wdn, xs, mods, g1, b1, alpha=alpha, name=f"ffn_down_{tag}")
            if is_ctx:
                xc = y
            else:
                xl = y
    return xl
```
